```python
import math
import jax, jax.numpy as jnp
from jax import lax
import numpy as np

D_MODEL = 4096
BATCH = 4
SEQ = 4096
DEPTH = 2
DEC_BATCH = 16
DEC_SEQ = 64
PAST_LEN = 2048

CHUNK = 64
PE_DIM = 256
D_FF = 4 * D_MODEL
LN_EPS = 1e-5
RMS_EPS = 1e-6
NEG_INF = -1e30

SWA_HEADS = 32
SWA_KV_HEADS = 4
SWA_GROUP = SWA_HEADS // SWA_KV_HEADS
SWA_HEAD_DIM = 64
SWA_WINDOW = 128
SWA_BACK_CHUNKS = SWA_WINDOW // CHUNK
GLA_HEADS = 4
GLA_HEAD_K = 256
GLA_HEAD_V = 512
GLA_RANK = 16
GLA_TAU = 16.0
GDN_HEADS = 16
GDN_HEAD_K = 128
GDN_HEAD_V = 128
GDN_CONV = 4

SWA_Q_W = SWA_HEADS * SWA_HEAD_DIM
SWA_KV_W = SWA_KV_HEADS * SWA_HEAD_DIM
GLA_K_W = GLA_HEADS * GLA_HEAD_K
GLA_V_W = GLA_HEADS * GLA_HEAD_V
GDN_K_W = GDN_HEADS * GDN_HEAD_K
GDN_V_W = GDN_HEADS * GDN_HEAD_V
GDN_CONV_CH = 2 * GDN_K_W + GDN_V_W
N_BRANCH = 3
IN_SIZES = (SWA_Q_W, SWA_KV_W, SWA_KV_W, GLA_K_W, GLA_K_W, GLA_V_W, GLA_RANK, GLA_V_W,
            GDN_CONV_CH, GDN_HEADS, GDN_HEADS, GDN_V_W, N_BRANCH * D_MODEL)
IN_COLS = sum(IN_SIZES)

DN_ALPHA = (2 * DEPTH) ** 0.25
DN_BETA = (8 * DEPTH) ** -0.25

kernel_name = 'hybrid_swa_gla_gdn_stream_step'


def _split_points():
    return tuple(int(c) for c in np.cumsum(IN_SIZES)[:-1])


def layer_norm(x, g, b):
    xf = x.astype(jnp.float32)
    mu = jnp.mean(xf, -1, keepdims=True)
    var = jnp.mean(jnp.square(xf - mu), -1, keepdims=True)
    return ((xf - mu) * lax.rsqrt(var + LN_EPS) * g.astype(jnp.float32) + b.astype(jnp.float32)).astype(x.dtype)


def rms_norm(x, g):
    xf = x.astype(jnp.float32)
    return xf * lax.rsqrt(jnp.mean(xf * xf, -1, keepdims=True) + RMS_EPS) * g.astype(jnp.float32)


def l2_normalize(x):
    return x * lax.rsqrt(jnp.sum(x * x, -1, keepdims=True) + 1e-6)


def alibi_slopes():
    return 2.0 ** (-8.0 * jnp.arange(1, SWA_HEADS + 1, dtype=jnp.float32) / SWA_HEADS)


def _to_blocks(t, blk):
    B, L = t.shape[:2]
    return jnp.moveaxis(t.reshape(B, L // blk, blk, *t.shape[2:]), 1, 0)


def _from_blocks(o):
    n, B, blk = o.shape[:3]
    return jnp.moveaxis(o, 0, 1).reshape(B, n * blk, *o.shape[3:])


def sink_attention(q, k, v, dist, valid, sinks):
    f32 = jnp.float32
    s = jnp.einsum('bnqkgd,bnskd->bnkgqs', q, k).astype(f32) * (SWA_HEAD_DIM ** -0.5)
    s = s - alibi_slopes().reshape(SWA_KV_HEADS, SWA_GROUP, 1, 1) * dist
    if valid is not None:
        s = jnp.where(valid[None, :, None, None], s, NEG_INF)
    sink = jnp.broadcast_to(sinks.astype(f32).reshape(SWA_KV_HEADS, SWA_GROUP, 1, 1), s.shape[:-1] + (1,))
    p = jax.nn.softmax(jnp.concatenate([s, sink], axis=-1), axis=-1)[..., :-1]
    return jnp.einsum('bnkgqs,bnskd->bnqkgd', p.astype(v.dtype), v)


def swa_prompt(q, k, v, sinks):
    B, S = q.shape[:2]
    n = S // CHUNK
    back = SWA_BACK_CHUNKS
    pad = ((0, 0), (back * CHUNK, 0), (0, 0), (0, 0))
    kc = jnp.pad(k, pad).reshape(B, n + back, CHUNK, SWA_KV_HEADS, SWA_HEAD_DIM)
    vc = jnp.pad(v, pad).reshape(B, n + back, CHUNK, SWA_KV_HEADS, SWA_HEAD_DIM)
    kb = jnp.concatenate([kc[:, j:j + n] for j in range(back + 1)], axis=2)
    vb = jnp.concatenate([vc[:, j:j + n] for j in range(back + 1)], axis=2)
    qb = q.reshape(B, n, CHUNK, SWA_KV_HEADS, SWA_GROUP, SWA_HEAD_DIM)
    qpos = jnp.arange(CHUNK) + back * CHUNK
    kpos = jnp.arange((back + 1) * CHUNK)
    dist = jnp.abs(qpos[:, None] - kpos[None, :]).astype(jnp.float32)
    valid = (jnp.arange(n)[:, None] * CHUNK + kpos[None, :] >= back * CHUNK)[:, None, :]
    o = sink_attention(qb, kb, vb, dist, valid, sinks)
    return o.reshape(B, S, SWA_Q_W)


def swa_sample(q, k, v, k_cache, v_cache, sinks):
    B, L = q.shape[:2]
    W = k_cache.shape[1]
    kk = jnp.concatenate([k_cache.astype(k.dtype), k], axis=1)
    vv = jnp.concatenate([v_cache.astype(v.dtype), v], axis=1)
    qb = q.reshape(B, 1, L, SWA_KV_HEADS, SWA_GROUP, SWA_HEAD_DIM)
    dist = jnp.abs((jnp.arange(L) + W)[:, None] - jnp.arange(W + L)[None, :]).astype(jnp.float32)
    o = sink_attention(qb, kk[:, None], vv[:, None], dist, None, sinks)
    return o.reshape(B, L, SWA_Q_W), kk[:, L:], vv[:, L:]


def gla_scan(q, k, v, log_a, s0):
    L = q.shape[1]
    blk = min(CHUNK, L)
    causal = jnp.tril(jnp.ones((blk, blk), bool))

    def step(S, inp):
        qb, kb, vb, gb = inp
        b = jnp.cumsum(gb, axis=1)
        qd = qb * jnp.exp(b)
        kd = kb * jnp.exp(-b)
        att = jnp.where(causal, jnp.einsum('bthk,bshk->bhts', qd, kd), 0.0)
        o = jnp.einsum('bhts,bshv->bthv', att, vb) + jnp.einsum('bthk,bhkv->bthv', qd, S)
        bl = b[:, -1]
        S = S * jnp.exp(bl)[..., None] + jnp.einsum('bshk,bshv->bhkv', kb * jnp.exp(bl[:, None] - b), vb)
        return S, o

    S, o = lax.scan(step, s0, (_to_blocks(q, blk), _to_blocks(k, blk), _to_blocks(v, blk), _to_blocks(log_a, blk)))
    return _from_blocks(o), S


def gdn_scan(q, k, v, g, beta, s0):
    L = q.shape[1]
    blk = min(CHUNK, L)
    eye = jnp.eye(blk, dtype=jnp.float32)
    lower = jnp.tril(jnp.ones((blk, blk), bool))
    strict = jnp.tril(jnp.ones((blk, blk), bool), -1)

    def step(S, inp):
        qb, kb, vb, gb, bb = inp
        gc = jnp.cumsum(gb, axis=1)
        gh = jnp.swapaxes(gc, 1, 2)
        decay = jnp.exp(jnp.where(lower, gh[..., :, None] - gh[..., None, :], -jnp.inf))
        bh = jnp.swapaxes(bb, 1, 2)
        kk = jnp.einsum('bthk,bshk->bhts', kb, kb)
        a = kk * jnp.where(strict, decay, 0.0) * bh[..., :, None]
        T = lax.linalg.triangular_solve(eye + a, jnp.broadcast_to(eye, a.shape), left_side=True, lower=True)
        u = jnp.einsum('bhts,bshv->bthv', T, vb * bb[..., None])
        w = jnp.einsum('bhts,bshk->bthk', T, kb * (bb * jnp.exp(gc))[..., None])
        v_new = u - jnp.einsum('bthk,bhkv->bthv', w, S)
        qk = jnp.einsum('bthk,bshk->bhts', qb, kb) * decay
        o = jnp.einsum('bthk,bhkv->bthv', qb * jnp.exp(gc)[..., None], S) + jnp.einsum('bhts,bshv->bthv', qk, v_new)
        gl = gc[:, -1]
        S = S * jnp.exp(gl)[..., None, None] + jnp.einsum('bshk,bshv->bhkv', kb * jnp.exp(gl[:, None] - gc)[..., None], v_new)
        return S, o

    S, o = lax.scan(step, s0, (_to_blocks(q, blk), _to_blocks(k, blk), _to_blocks(v, blk), _to_blocks(g, blk), _to_blocks(beta, blk)))
    return _from_blocks(o), S


def gla_mixer(q, k, v, lr, r, w_gate2, gate_bias, norm_w, s0):
    f32 = jnp.float32
    B, L = q.shape[:2]
    q = q.astype(f32).reshape(B, L, GLA_HEADS, GLA_HEAD_K) * (GLA_HEAD_K ** -0.5)
    k = k.astype(f32).reshape(B, L, GLA_HEADS, GLA_HEAD_K)
    v = v.astype(f32).reshape(B, L, GLA_HEADS, GLA_HEAD_V)
    log_a = jax.nn.log_sigmoid(lr.astype(f32) @ w_gate2.astype(f32) + gate_bias.astype(f32)) / GLA_TAU
    o, S = gla_scan(q, k, v, log_a.reshape(B, L, GLA_HEADS, GLA_HEAD_K), s0.astype(f32))
    o = rms_norm(o, norm_w) * jax.nn.silu(r.astype(f32).reshape(B, L, GLA_HEADS, GLA_HEAD_V))
    return o.reshape(B, L, GLA_V_W), S


def gdn_mixer(qkv, b_raw, a_raw, z, conv_buf, conv_w, a_log, dt_bias, norm_w, s0):
    f32 = jnp.float32
    B, L = qkv.shape[:2]
    xc = jnp.concatenate([conv_buf.astype(qkv.dtype), qkv], axis=1)
    new_buf = xc[:, L:]
    conv = lax.conv_general_dilated(xc.astype(f32), conv_w.astype(f32).reshape(GDN_CONV, 1, GDN_CONV_CH),
                                    window_strides=(1,), padding='VALID',
                                    dimension_numbers=('NWC', 'WIO', 'NWC'), feature_group_count=GDN_CONV_CH)
    conv = jax.nn.silu(conv)
    q, k, v = jnp.split(conv, (GDN_K_W, 2 * GDN_K_W), axis=-1)
    q = l2_normalize(q.reshape(B, L, GDN_HEADS, GDN_HEAD_K)) * (GDN_HEAD_K ** -0.5)
    k = l2_normalize(k.reshape(B, L, GDN_HEADS, GDN_HEAD_K))
    v = v.reshape(B, L, GDN_HEADS, GDN_HEAD_V)
    beta = jax.nn.sigmoid(b_raw.astype(f32))
    g = -jnp.exp(a_log.astype(f32)) * jax.nn.softplus(a_raw.astype(f32) + dt_bias.astype(f32))
    o, S = gdn_scan(q, k, v, g, beta, s0.astype(f32))
    o = rms_norm(o, norm_w) * jax.nn.silu(z.astype(f32).reshape(B, L, GDN_HEADS, GDN_HEAD_V))
    return o.reshape(B, L, GDN_V_W), S, new_buf


def trunk_layer(x, pe, W, swa_cache, gla_s0, gdn_s0, conv_buf):
    B, L, _ = x.shape
    dt = x.dtype
    proj = x @ W['w_in']
    (sq, sk, sv, gq, gk, gv, glr, gr, dqkv, db, da, dz, mg) = jnp.split(proj, _split_points(), axis=-1)
    sq = sq.reshape(B, L, SWA_HEADS, SWA_HEAD_DIM)
    sk = sk.reshape(B, L, SWA_KV_HEADS, SWA_HEAD_DIM)
    sv = sv.reshape(B, L, SWA_KV_HEADS, SWA_HEAD_DIM)
    if swa_cache is None:
        o_a = swa_prompt(sq, sk, sv, W['swa_sinks'])
        n_keep = min(SWA_WINDOW, L)
        new_k, new_v = sk[:, L - n_keep:], sv[:, L - n_keep:]
    else:
        o_a, new_k, new_v = swa_sample(sq, sk, sv, swa_cache[0], swa_cache[1], W['swa_sinks'])
    o_b, gla_s = gla_mixer(gq, gk, gv, glr, gr, W['gla_w_gate2'], W['gla_gate_bias'], W['gla_norm_w'], gla_s0)
    o_c, gdn_s, new_buf = gdn_mixer(dqkv, db, da, dz, conv_buf, W['gdn_conv_w'], W['gdn_a_log'],
                                    W['gdn_dt_bias'], W['gdn_norm_w'], gdn_s0)
    gates = jax.nn.sigmoid(mg.astype(jnp.float32)).reshape(B, L, N_BRANCH, D_MODEL)
    merged = (gates[:, :, 0] * (o_a.astype(dt) @ W['w_br_swa'])
              + gates[:, :, 1] * (o_b.astype(dt) @ W['w_br_gla'])
              + gates[:, :, 2] * (o_c.astype(dt) @ W['w_br_gdn']))
    x = layer_norm(DN_ALPHA * x + merged.astype(dt) @ W['w_out'], W['ln1_g'], W['ln1_b'])
    x = layer_norm(DN_ALPHA * x + jnp.square(jax.nn.relu(x @ W['w_up'])) @ W['w_down'], W['ln2_g'], W['ln2_b'])
    x = layer_norm(DN_ALPHA * x + jax.nn.sigmoid(x @ W['pe_w_gate']) * (pe @ W['pe_w_proj']), W['ln3_g'], W['ln3_b'])
    return x, (new_k, new_v, gla_s, gdn_s, new_buf)


def setup_inputs(seed: int = 0) -> dict:
    key = jax.random.key(seed)
    ks = iter(jax.random.split(key, 48))

    def nrm(shape, scale):
        return jax.random.normal(next(ks), shape, jnp.float32) * scale

    n_win = min(SWA_WINDOW, PAST_LEN)
    dt0 = jnp.exp(jax.random.uniform(next(ks), (DEPTH, GDN_HEADS), jnp.float32, math.log(1e-3), math.log(1e-1)))
    return {
        'x_prompt': nrm((BATCH, SEQ, D_MODEL), 1.0),
        'x_sample': nrm((DEC_BATCH, DEC_SEQ, D_MODEL), 1.0),
        'cache_swa_k': nrm((DEPTH, DEC_BATCH, n_win, SWA_KV_HEADS, SWA_HEAD_DIM), 1.0),
        'cache_swa_v': nrm((DEPTH, DEC_BATCH, n_win, SWA_KV_HEADS, SWA_HEAD_DIM), 1.0),
        'state_gla': nrm((DEPTH, DEC_BATCH, GLA_HEADS, GLA_HEAD_K, GLA_HEAD_V), 0.1),
        'state_gdn': nrm((DEPTH, DEC_BATCH, GDN_HEADS, GDN_HEAD_K, GDN_HEAD_V), 0.1),
        'state_gdn_conv': nrm((DEPTH, DEC_BATCH, GDN_CONV - 1, GDN_CONV_CH), 1.0),
        'p_prompt': nrm((DEPTH, BATCH, SEQ, PE_DIM), 1.0),
        'p_sample': nrm((DEPTH, DEC_BATCH, DEC_SEQ, PE_DIM), 1.0),
        'w_in': nrm((DEPTH, D_MODEL, IN_COLS), D_MODEL ** -0.5),
        'swa_sinks': nrm((DEPTH, SWA_HEADS), 1.0),
        'gla_w_gate2': nrm((DEPTH, GLA_RANK, GLA_K_W), GLA_RANK ** -0.5),
        'gla_gate_bias': nrm((DEPTH, GLA_K_W), 0.1),
        'gla_norm_w': 1.0 + nrm((DEPTH, GLA_HEAD_V), 0.02),
        'gdn_conv_w': nrm((DEPTH, GDN_CONV, GDN_CONV_CH), GDN_CONV ** -0.5),
        'gdn_a_log': jnp.log(jax.random.uniform(next(ks), (DEPTH, GDN_HEADS), jnp.float32, 1.0, 16.0)),
        'gdn_dt_bias': dt0 + jnp.log(-jnp.expm1(-dt0)),
        'gdn_norm_w': 1.0 + nrm((DEPTH, GDN_HEAD_V), 0.02),
        'w_br_swa': nrm((DEPTH, SWA_Q_W, D_MODEL), SWA_Q_W ** -0.5),
        'w_br_gla': nrm((DEPTH, GLA_V_W, D_MODEL), GLA_V_W ** -0.5),
        'w_br_gdn': nrm((DEPTH, GDN_V_W, D_MODEL), GDN_V_W ** -0.5),
        'w_out': nrm((DEPTH, D_MODEL, D_MODEL), DN_BETA * D_MODEL ** -0.5),
        'ln1_g': 1.0 + nrm((DEPTH, D_MODEL), 0.02),
        'ln1_b': nrm((DEPTH, D_MODEL), 0.02),
        'w_up': nrm((DEPTH, D_MODEL, D_FF), D_MODEL ** -0.5),
        'w_down': nrm((DEPTH, D_FF, D_MODEL), DN_BETA * D_FF ** -0.5),
        'ln2_g': 1.0 + nrm((DEPTH, D_MODEL), 0.02),
        'ln2_b': nrm((DEPTH, D_MODEL), 0.02),
        'pe_w_gate': nrm((DEPTH, D_MODEL, D_MODEL), D_MODEL ** -0.5),
        'pe_w_proj': nrm((DEPTH, PE_DIM, D_MODEL), DN_BETA * PE_DIM ** -0.5),
        'ln3_g': 1.0 + nrm((DEPTH, D_MODEL), 0.02),
        'ln3_b': nrm((DEPTH, D_MODEL), 0.02),
    }


def reference(x_prompt, x_sample, cache_swa_k, cache_swa_v, state_gla, state_gdn, state_gdn_conv,
              p_prompt, p_sample, w_in, swa_sinks, gla_w_gate2, gla_gate_bias, gla_norm_w,
              gdn_conv_w, gdn_a_log, gdn_dt_bias, gdn_norm_w, w_br_swa, w_br_gla, w_br_gdn, w_out,
              ln1_g, ln1_b, w_up, w_down, ln2_g, ln2_b, pe_w_gate, pe_w_proj, ln3_g, ln3_b):
    f32 = jnp.float32
    bp = x_prompt.shape[0]
    yp, ys = x_prompt, x_sample
    st_p, st_s = [], []
    for l in range(DEPTH):
        W = {'w_in': w_in[l], 'swa_sinks': swa_sinks[l], 'gla_w_gate2': gla_w_gate2[l],
             'gla_gate_bias': gla_gate_bias[l], 'gla_norm_w': gla_norm_w[l], 'gdn_conv_w': gdn_conv_w[l],
             'gdn_a_log': gdn_a_log[l], 'gdn_dt_bias': gdn_dt_bias[l], 'gdn_norm_w': gdn_norm_w[l],
             'w_br_swa': w_br_swa[l], 'w_br_gla': w_br_gla[l], 'w_br_gdn': w_br_gdn[l], 'w_out': w_out[l],
             'ln1_g': ln1_g[l], 'ln1_b': ln1_b[l], 'w_up': w_up[l], 'w_down': w_down[l],
             'ln2_g': ln2_g[l], 'ln2_b': ln2_b[l], 'pe_w_gate': pe_w_gate[l], 'pe_w_proj': pe_w_proj[l],
             'ln3_g': ln3_g[l], 'ln3_b': ln3_b[l]}
        yp, sp = trunk_layer(yp, p_prompt[l], W, None,
                             jnp.zeros((bp, GLA_HEADS, GLA_HEAD_K, GLA_HEAD_V), f32),
                             jnp.zeros((bp, GDN_HEADS, GDN_HEAD_K, GDN_HEAD_V), f32),
                             jnp.zeros((bp, GDN_CONV - 1, GDN_CONV_CH), x_prompt.dtype))
        ys, ss = trunk_layer(ys, p_sample[l], W, (cache_swa_k[l], cache_swa_v[l]),
                             state_gla[l], state_gdn[l], state_gdn_conv[l])
        st_p.append(sp)
        st_s.append(ss)
    swa_k_prompt = jnp.stack([s[0] for s in st_p])
    swa_v_prompt = jnp.stack([s[1] for s in st_p])
    gla_prompt = jnp.stack([s[2] for s in st_p])
    gdn_prompt = jnp.stack([s[3] for s in st_p])
    gdn_conv_prompt = jnp.stack([s[4] for s in st_p])
    swa_k_sample = jnp.stack([s[0] for s in st_s])
    swa_v_sample = jnp.stack([s[1] for s in st_s])
    gla_sample = jnp.stack([s[2] for s in st_s])
    gdn_sample = jnp.stack([s[3] for s in st_s])
    gdn_conv_sample = jnp.stack([s[4] for s in st_s])
    return (yp, ys, swa_k_prompt, swa_v_prompt, gla_prompt, gdn_prompt, gdn_conv_prompt,
            swa_k_sample, swa_v_sample, gla_sample, gdn_sample, gdn_conv_sample)
```

```python
import functools

import jax
import jax.numpy as jnp
from jax import lax
from jax.experimental import pallas as pl
from jax.experimental.pallas import tpu as pltpu

F32 = jnp.float32
BF16 = jnp.bfloat16
HIGHEST = lax.Precision.HIGHEST

D_MODEL = 4096
DEPTH = 2
CHUNK = 64
PE_DIM = 256
D_FF = 4 * D_MODEL
LN_EPS = 1e-5
RMS_EPS = 1e-6
NEG_INF = -1e30

SWA_HEADS = 32
SWA_KV_HEADS = 4
SWA_GROUP = SWA_HEADS // SWA_KV_HEADS
SWA_HEAD_DIM = 64
SWA_WINDOW = 128
SWA_KEYS = SWA_WINDOW + CHUNK
GLA_HEADS = 4
GLA_HEAD_K = 256
GLA_HEAD_V = 512
GLA_RANK = 16
GLA_TAU = 16.0
GDN_HEADS = 16
GDN_HEAD_K = 128
GDN_HEAD_V = 128
GDN_CONV = 4

SWA_Q_W = SWA_HEADS * SWA_HEAD_DIM
SWA_KV_W = SWA_KV_HEADS * SWA_HEAD_DIM
GLA_K_W = GLA_HEADS * GLA_HEAD_K
GLA_V_W = GLA_HEADS * GLA_HEAD_V
GDN_K_W = GDN_HEADS * GDN_HEAD_K
GDN_V_W = GDN_HEADS * GDN_HEAD_V
GDN_CONV_CH = 2 * GDN_K_W + GDN_V_W
N_BRANCH = 3
IN_SIZES = (SWA_Q_W, SWA_KV_W, SWA_KV_W, GLA_K_W, GLA_K_W, GLA_V_W, GLA_RANK, GLA_V_W,
            GDN_CONV_CH, GDN_HEADS, GDN_HEADS, GDN_V_W, N_BRANCH * D_MODEL)

DN_ALPHA = (2 * DEPTH) ** 0.25

LANES = 128
SUBLANES = 8
VMEM_LIMIT_BYTES = 56 * 1024 * 1024
SLAB_GLR = 0
SLAB_BETA = 16
SLAB_DECAY = 32
CONV_PAD_ROWS = SUBLANES


def _params(*sem):
    return pltpu.CompilerParams(dimension_semantics=sem, vmem_limit_bytes=VMEM_LIMIT_BYTES)


def _row_tile(m, want):
    t = want
    while m % t:
        t //= 2
    assert t >= SUBLANES
    return t


def _dot(a, b):
    return jnp.dot(a, b, preferred_element_type=F32)


def _dot_nt(a, b):
    return lax.dot_general(a, b, (((1,), (1,)), ((), ())), preferred_element_type=F32)


def _dot_tn(a, b):
    return lax.dot_general(a, b, (((0,), (0,)), ((), ())), preferred_element_type=F32)


def _sigmoid(x):
    return 1.0 / (1.0 + jnp.exp(-x))


def _silu(x):
    return x * _sigmoid(x)


def _softplus(x):
    return jnp.maximum(x, 0.0) + jnp.log(1.0 + jnp.exp(-jnp.abs(x)))


def _mm_body(*refs, nk, n_extra, n_out, epilogue):
    a_ref, w_ref = refs[0], refs[1]
    extra = refs[2:2 + n_extra]
    outs = refs[2 + n_extra:2 + n_extra + n_out]

    def finish(acc):
        vals = epilogue(acc, *extra)
        for o, v in zip(outs, vals):
            o[...] = v.astype(o.dtype)

    if nk == 1:
        finish(_dot(a_ref[...], w_ref[...]))
    else:
        acc_ref = refs[-1]
        k = pl.program_id(2)

        @pl.when(k == 0)
        def _():
            acc_ref[...] = jnp.zeros_like(acc_ref)

        acc_ref[...] += _dot(a_ref[...], w_ref[...])

        @pl.when(k == nk - 1)
        def _():
            finish(acc_ref[...])


def _matmul(a, w, out_dtypes, epilogue, extras=(), *, tm=1024, tn=1024, tk=None, name):
    M, K = a.shape
    N = w.shape[1]
    tm = _row_tile(M, tm)
    tn = min(tn, N)
    tk = K if tk is None else tk
    assert M % tm == 0 and N % tn == 0 and K % tk == 0
    nk = K // tk
    in_specs = [pl.BlockSpec((tm, tk), lambda i, j, k: (i, k)),
                pl.BlockSpec((tk, tn), lambda i, j, k: (k, j))]
    args = [a, w]
    for arr, block, imap in extras:
        in_specs.append(pl.BlockSpec(block, imap))
        args.append(arr)
    out_specs = [pl.BlockSpec((tm, tn), lambda i, j, k: (i, j)) for _ in out_dtypes]
    out_shape = [jax.ShapeDtypeStruct((M, N), dt) for dt in out_dtypes]
    scratch = [pltpu.VMEM((tm, tn), F32)] if nk > 1 else []
    body = functools.partial(_mm_body, nk=nk, n_extra=len(extras), n_out=len(out_dtypes), epilogue=epilogue)
    return pl.pallas_call(
        body, grid=(M // tm, N // tn, nk), in_specs=in_specs, out_specs=out_specs, out_shape=out_shape,
        scratch_shapes=scratch, compiler_params=_params("parallel", "parallel", "arbitrary"), name=name,
    )(*args)


def _tile_spec(tm, tn):
    return (tm, tn), (lambda i, j, k: (i, j))


def _ep_identity(acc):
    return (acc,)


def _ep_relu_sq(acc):
    r = jnp.maximum(acc, 0.0)
    return (r * r,)


def _ep_residual(acc, x_ref):
    return (DN_ALPHA * x_ref[...] + acc,)


def _ep_pe(acc, x_ref, pe_ref, wp_ref):
    proj = _dot(pe_ref[...], wp_ref[...])
    return (DN_ALPHA * x_ref[...] + _sigmoid(acc) * proj,)


def _merge_body(o_ref, w_ref, g_ref, out_ref, acc_ref):
    b = pl.program_id(2)
    contrib = _sigmoid(g_ref[...]) * _dot(o_ref[...], w_ref[...])

    @pl.when(b == 0)
    def _():
        acc_ref[...] = contrib

    @pl.when(b > 0)
    def _():
        acc_ref[...] += contrib

    @pl.when(b == N_BRANCH - 1)
    def _():
        out_ref[...] = acc_ref[...].astype(out_ref.dtype)


def _merge(o_stack, w_stack, gates, *, tm=1024, tn=1024):
    _, M, K = o_stack.shape
    N = w_stack.shape[2]
    tm = _row_tile(M, tm)
    nj = N // tn
    return pl.pallas_call(
        _merge_body, grid=(M // tm, nj, N_BRANCH),
        in_specs=[pl.BlockSpec((None, tm, K), lambda i, j, b: (b, i, 0)),
                  pl.BlockSpec((None, K, tn), lambda i, j, b: (b, 0, j)),
                  pl.BlockSpec((tm, tn), lambda i, j, b: (i, b * nj + j))],
        out_specs=pl.BlockSpec((tm, tn), lambda i, j, b: (i, j)),
        out_shape=jax.ShapeDtypeStruct((M, N), BF16),
        scratch_shapes=[pltpu.VMEM((tm, tn), F32)],
        compiler_params=_params("parallel", "parallel", "arbitrary"), name="merge_branches",
    )(o_stack, w_stack, gates)


def _ln_body(z_ref, g_ref, b_ref, x_ref, xb_ref):
    z = z_ref[...]
    mu = jnp.mean(z, axis=-1, keepdims=True)
    d = z - mu
    var = jnp.mean(d * d, axis=-1, keepdims=True)
    y = d * lax.rsqrt(var + LN_EPS) * g_ref[...] + b_ref[...]
    x_ref[...] = y
    xb_ref[...] = y.astype(BF16)


def _layer_norm(z, g, b, *, tm=256):
    M, D = z.shape
    tm = _row_tile(M, tm)
    row = pl.BlockSpec((tm, D), lambda i: (i, 0))
    vec = pl.BlockSpec((1, D), lambda i: (0, 0))
    return pl.pallas_call(
        _ln_body, grid=(M // tm,), in_specs=[row, vec, vec], out_specs=[row, row],
        out_shape=[jax.ShapeDtypeStruct((M, D), F32), jax.ShapeDtypeStruct((M, D), BF16)],
        compiler_params=_params("parallel"), name="layer_norm",
    )(z, g.reshape(1, D), b.reshape(1, D))


def _swa_body(q_ref, k0_ref, k1_ref, k2_ref, v0_ref, v1_ref, v2_ref, bias_ref, sink_ref, o_ref, *, mask_front):
    c = pl.program_id(1)
    rows = CHUNK * SWA_GROUP
    for j in range(SWA_KV_HEADS):
        q = (q_ref[:, j] * (SWA_HEAD_DIM ** -0.5)).reshape(rows, SWA_HEAD_DIM).astype(BF16)
        k = jnp.concatenate([k0_ref[j], k1_ref[j], k2_ref[j]], axis=0)
        v = jnp.concatenate([v0_ref[j], v1_ref[j], v2_ref[j]], axis=0)
        s = _dot_nt(q, k) - bias_ref[j]
        if mask_front:
            col = lax.broadcasted_iota(jnp.int32, s.shape, 1)
            s = jnp.where(col + c * CHUNK >= SWA_WINDOW, s, NEG_INF)
        sink = sink_ref[j]
        m = jnp.maximum(jnp.max(s, axis=-1, keepdims=True), sink)
        p = jnp.exp(s - m)
        denom = jnp.sum(p, axis=-1, keepdims=True) + jnp.exp(sink - m)
        o = _dot(p.astype(BF16), v) / denom
        o_ref[:, j] = o.reshape(CHUNK, SWA_GROUP, SWA_HEAD_DIM)


def _swa(q4, k_win, v_win, bias, sink, *, batch, n_chunks, row_block0, mask_front):
    q_spec = pl.BlockSpec((CHUNK, SWA_KV_HEADS, SWA_GROUP, SWA_HEAD_DIM),
                          lambda b, c: (row_block0 + b * n_chunks + c, 0, 0, 0))

    def win(w):
        return pl.BlockSpec((None, SWA_KV_HEADS, CHUNK, SWA_HEAD_DIM), lambda b, c: (b, 0, c + w, 0))

    rows = CHUNK * SWA_GROUP
    bias_spec = pl.BlockSpec((SWA_KV_HEADS, rows, SWA_KEYS), lambda b, c: (0, 0, 0))
    sink_spec = pl.BlockSpec((SWA_KV_HEADS, rows, 1), lambda b, c: (0, 0, 0))
    out_spec = pl.BlockSpec((CHUNK, SWA_KV_HEADS, SWA_GROUP, SWA_HEAD_DIM), lambda b, c: (b * n_chunks + c, 0, 0, 0))
    return pl.pallas_call(
        functools.partial(_swa_body, mask_front=mask_front), grid=(batch, n_chunks),
        in_specs=[q_spec, win(0), win(1), win(2), win(0), win(1), win(2), bias_spec, sink_spec],
        out_specs=out_spec,
        out_shape=jax.ShapeDtypeStruct((batch * n_chunks * CHUNK, SWA_KV_HEADS, SWA_GROUP, SWA_HEAD_DIM), F32),
        compiler_params=_params("parallel", "arbitrary"), name="swa_chunk",
    )(q4, k_win, k_win, k_win, v_win, v_win, v_win, bias, sink)


def _gla_body(q_ref, k_ref, v_ref, slab_ref, r_ref, wg_ref, gb_ref, nw_ref, s0_ref, o_ref, state_ref):
    c = pl.program_id(2)

    @pl.when(c == 0)
    def _():
        state_ref[...] = s0_ref[...]

    row = lax.broadcasted_iota(jnp.int32, (CHUNK, CHUNK), 0)
    col = lax.broadcasted_iota(jnp.int32, (CHUNK, CHUNK), 1)
    causal = row >= col
    tril = causal.astype(F32)

    ga = jnp.dot(slab_ref[...], wg_ref[...], precision=HIGHEST, preferred_element_type=F32) + gb_ref[...]
    log_a = (jnp.minimum(ga, 0.0) - jnp.log(1.0 + jnp.exp(-jnp.abs(ga)))) * (1.0 / GLA_TAU)
    b = jnp.dot(tril, log_a, precision=HIGHEST, preferred_element_type=F32)
    b_last = b[CHUNK - 1:CHUNK, :]
    ones = jnp.ones((CHUNK, LANES), F32)
    b_last_col = lax.dot_general(log_a, ones, (((0,), (0,)), ((), ())), precision=HIGHEST,
                                 preferred_element_type=F32)[:, :1]

    q = q_ref[...] * (GLA_HEAD_K ** -0.5)
    k = k_ref[...]
    v = v_ref[...].astype(BF16)
    qd = (q * jnp.exp(b)).astype(BF16)
    kd = (k * jnp.exp(-b)).astype(BF16)
    att = jnp.where(causal, _dot_nt(qd, kd), 0.0)
    state = state_ref[...]
    o = _dot(att.astype(BF16), v) + _dot(qd, state.astype(BF16))
    k_tail = (k * jnp.exp(b_last - b)).astype(BF16)
    state_ref[...] = state * jnp.exp(b_last_col) + _dot_tn(k_tail, v)

    ms = jnp.mean(o * o, axis=-1, keepdims=True)
    o = o * lax.rsqrt(ms + RMS_EPS) * nw_ref[...] * _silu(r_ref[...])
    o_ref[...] = o.astype(o_ref.dtype)


def _gla(qkv, slab, gate_r, wg_pad, gate_bias, norm_w, s0, *, batch, n_chunks, row_block0):
    def rows(b, c):
        return row_block0 + b * n_chunks + c

    kb = GLA_K_W // GLA_HEAD_K
    vb = 2 * GLA_K_W // GLA_HEAD_V
    in_specs = [
        pl.BlockSpec((CHUNK, GLA_HEAD_K), lambda b, h, c: (rows(b, c), h)),
        pl.BlockSpec((CHUNK, GLA_HEAD_K), lambda b, h, c: (rows(b, c), kb + h)),
        pl.BlockSpec((CHUNK, GLA_HEAD_V), lambda b, h, c: (rows(b, c), vb + h)),
        pl.BlockSpec((CHUNK, LANES), lambda b, h, c: (rows(b, c), 0)),
        pl.BlockSpec((CHUNK, GLA_HEAD_V), lambda b, h, c: (rows(b, c), h)),
        pl.BlockSpec((LANES, GLA_HEAD_K), lambda b, h, c: (0, h)),
        pl.BlockSpec((1, GLA_HEAD_K), lambda b, h, c: (0, h)),
        pl.BlockSpec((1, GLA_HEAD_V), lambda b, h, c: (0, 0)),
        pl.BlockSpec((None, None, GLA_HEAD_K, GLA_HEAD_V), lambda b, h, c: (b, h, 0, 0)),
    ]
    out_specs = [
        pl.BlockSpec((CHUNK, GLA_HEAD_V), lambda b, h, c: (b * n_chunks + c, h)),
        pl.BlockSpec((None, None, GLA_HEAD_K, GLA_HEAD_V), lambda b, h, c: (b, h, 0, 0)),
    ]
    out_shape = [jax.ShapeDtypeStruct((batch * n_chunks * CHUNK, GLA_V_W), BF16),
                 jax.ShapeDtypeStruct((batch, GLA_HEADS, GLA_HEAD_K, GLA_HEAD_V), F32)]
    return pl.pallas_call(
        _gla_body, grid=(batch, GLA_HEADS, n_chunks), in_specs=in_specs, out_specs=out_specs, out_shape=out_shape,
        compiler_params=_params("parallel", "parallel", "arbitrary"), name="gla_chunk",
    )(qkv, qkv, qkv, slab, gate_r, wg_pad, gate_bias, norm_w, s0)


def _gdn_body(x_ref, slab_ref, z_ref, cw_ref, alog_ref, dtb_ref, nw_ref, conv0_ref, s0_ref,
              o_ref, conv_ref, state_ref, xc_ref):
    c = pl.program_id(1)

    @pl.when(c == 0)
    def _():
        conv_ref[...] = conv0_ref[...]
        state_ref[...] = s0_ref[...]

    x = x_ref[...]
    xc_ref[0:CONV_PAD_ROWS, :] = conv_ref[...]
    xc_ref[CONV_PAD_ROWS:CONV_PAD_ROWS + CHUNK, :] = x
    conv = cw_ref[GDN_CONV - 1:GDN_CONV, :] * x
    for d in range(1, GDN_CONV):
        conv = conv + cw_ref[GDN_CONV - 1 - d:GDN_CONV - d, :] * xc_ref[CONV_PAD_ROWS - d:CONV_PAD_ROWS - d + CHUNK, :]
    conv_ref[...] = x[CHUNK - CONV_PAD_ROWS:, :]
    y = _silu(conv)

    row = lax.broadcasted_iota(jnp.int32, (CHUNK, CHUNK), 0)
    col = lax.broadcasted_iota(jnp.int32, (CHUNK, CHUNK), 1)
    lower = row >= col
    strict = row > col
    eye = (row == col).astype(F32)
    tril = lower.astype(F32)

    slab = slab_ref[...]
    beta_all = _sigmoid(slab)
    lane = lax.broadcasted_iota(jnp.int32, slab.shape, 1)
    is_decay = (lane >= SLAB_DECAY) & (lane < SLAB_DECAY + GDN_HEADS)
    g_all = jnp.where(is_decay, -jnp.exp(alog_ref[...]) * _softplus(slab + dtb_ref[...]), 0.0)
    gc_all = jnp.dot(tril, g_all, precision=HIGHEST, preferred_element_type=F32)
    gc_rows = gc_all.T

    for h in range(GDN_HEADS):
        qh = y[:, h * GDN_HEAD_K:(h + 1) * GDN_HEAD_K]
        kh = y[:, GDN_K_W + h * GDN_HEAD_K:GDN_K_W + (h + 1) * GDN_HEAD_K]
        vh = y[:, 2 * GDN_K_W + h * GDN_HEAD_V:2 * GDN_K_W + (h + 1) * GDN_HEAD_V]
        qn = qh * lax.rsqrt(jnp.sum(qh * qh, axis=-1, keepdims=True) + 1e-6) * (GDN_HEAD_K ** -0.5)
        kn = kh * lax.rsqrt(jnp.sum(kh * kh, axis=-1, keepdims=True) + 1e-6)
        gcc = gc_all[:, SLAB_DECAY + h:SLAB_DECAY + h + 1]
        gcr = gc_rows[SLAB_DECAY + h:SLAB_DECAY + h + 1, :]
        beta = beta_all[:, SLAB_BETA + h:SLAB_BETA + h + 1]
        decay = jnp.exp(jnp.where(lower, gcc - gcr, -jnp.inf))
        kb = kn.astype(BF16)
        a = _dot_nt(kb, kb) * jnp.where(strict, decay, 0.0) * beta
        t = eye - a
        ap = a
        for _ in range(5):
            apb = ap.astype(BF16)
            ap = _dot(apb, apb)
            t = t + _dot(t.astype(BF16), ap.astype(BF16))
        rhs = jnp.concatenate([vh * beta, kn * (beta * jnp.exp(gcc))], axis=1).astype(BF16)
        uw = _dot(t.astype(BF16), rhs)
        u = uw[:, :GDN_HEAD_V]
        w = uw[:, GDN_HEAD_V:]
        state = state_ref[h]
        sb = state.astype(BF16)
        v_new = u - _dot(w.astype(BF16), sb)
        vnb = v_new.astype(BF16)
        qk = _dot_nt(qn.astype(BF16), kb) * decay
        o = _dot((qn * jnp.exp(gcc)).astype(BF16), sb) + _dot(qk.astype(BF16), vnb)
        g_last = gcc[CHUNK - 1:CHUNK, :]
        state_ref[h] = state * jnp.exp(g_last) + _dot_tn((kn * jnp.exp(g_last - gcc)).astype(BF16), vnb)

        ms = jnp.mean(o * o, axis=-1, keepdims=True)
        zh = z_ref[:, h * GDN_HEAD_V:(h + 1) * GDN_HEAD_V]
        o = o * lax.rsqrt(ms + RMS_EPS) * nw_ref[...] * _silu(zh)
        o_ref[:, h * GDN_HEAD_V:(h + 1) * GDN_HEAD_V] = o.astype(o_ref.dtype)


def _gdn(qkv, slab, z, conv_w, alog_row, dtb_row, norm_w, conv0, s0, *, batch, n_chunks, row_block0):
    def rows(b, c):
        return row_block0 + b * n_chunks + c

    in_specs = [
        pl.BlockSpec((CHUNK, GDN_CONV_CH), lambda b, c: (rows(b, c), 0)),
        pl.BlockSpec((CHUNK, LANES), lambda b, c: (rows(b, c), 0)),
        pl.BlockSpec((CHUNK, GDN_V_W), lambda b, c: (rows(b, c), 0)),
        pl.BlockSpec((GDN_CONV, GDN_CONV_CH), lambda b, c: (0, 0)),
        pl.BlockSpec((1, LANES), lambda b, c: (0, 0)),
        pl.BlockSpec((1, LANES), lambda b, c: (0, 0)),
        pl.BlockSpec((1, GDN_HEAD_V), lambda b, c: (0, 0)),
        pl.BlockSpec((None, CONV_PAD_ROWS, GDN_CONV_CH), lambda b, c: (b, 0, 0)),
        pl.BlockSpec((None, GDN_HEADS, GDN_HEAD_K, GDN_HEAD_V), lambda b, c: (b, 0, 0, 0)),
    ]
    out_specs = [
        pl.BlockSpec((CHUNK, GDN_V_W), lambda b, c: (b * n_chunks + c, 0)),
        pl.BlockSpec((None, CONV_PAD_ROWS, GDN_CONV_CH), lambda b, c: (b, 0, 0)),
        pl.BlockSpec((None, GDN_HEADS, GDN_HEAD_K, GDN_HEAD_V), lambda b, c: (b, 0, 0, 0)),
    ]
    out_shape = [jax.ShapeDtypeStruct((batch * n_chunks * CHUNK, GDN_V_W), BF16),
                 jax.ShapeDtypeStruct((batch, CONV_PAD_ROWS, GDN_CONV_CH), F32),
                 jax.ShapeDtypeStruct((batch, GDN_HEADS, GDN_HEAD_K, GDN_HEAD_V), F32)]
    return pl.pallas_call(
        _gdn_body, grid=(batch, n_chunks), in_specs=in_specs, out_specs=out_specs, out_shape=out_shape,
        scratch_shapes=[pltpu.VMEM((CONV_PAD_ROWS + CHUNK, GDN_CONV_CH), F32)],
        compiler_params=_params("parallel", "arbitrary"), name="gdn_chunk",
    )(qkv, slab, z, conv_w, alog_row, dtb_row, norm_w, conv0, s0)


def _split_w_in(w):
    edges = [0]
    for s in IN_SIZES:
        edges.append(edges[-1] + s)
    col = lambda a, b: w[:, edges[a]:edges[b]].astype(BF16)
    slab = jnp.zeros((D_MODEL, LANES), BF16)
    slab = slab.at[:, SLAB_GLR:SLAB_GLR + GLA_RANK].set(col(6, 7))
    slab = slab.at[:, SLAB_BETA:SLAB_BETA + GDN_HEADS].set(col(9, 10))
    slab = slab.at[:, SLAB_DECAY:SLAB_DECAY + GDN_HEADS].set(col(10, 11))
    return dict(swa_q=col(0, 1), swa_kv=col(1, 3), gla_qkv=col(3, 6), slab=slab, gla_r=col(7, 8),
                gdn_qkv=col(8, 9), gdn_z=col(11, 12), gates=col(12, 13))


def _swa_tables(sinks):
    rows = CHUNK * SWA_GROUP
    t = jnp.arange(rows) // SWA_GROUP
    g = jnp.arange(rows) % SWA_GROUP
    dist = jnp.abs((t + SWA_WINDOW)[:, None] - jnp.arange(SWA_KEYS)[None, :]).astype(F32)
    head = jnp.arange(SWA_KV_HEADS)[:, None] * SWA_GROUP + g[None, :]
    slopes = 2.0 ** (-8.0 * (head + 1).astype(F32) / SWA_HEADS)
    bias = slopes[:, :, None] * dist[None]
    sink = sinks.astype(F32)[head][:, :, None]
    return bias, sink


def _kv_windows(kv, batch, seq):
    t = kv.reshape(batch, seq, 2, SWA_KV_HEADS, SWA_HEAD_DIM)
    t = jnp.transpose(t, (2, 0, 3, 1, 4))
    return t[0], t[1]


def _pad_rows(t, n_rows):
    return jnp.pad(t, ((0, 0), (n_rows - t.shape[1], 0), (0, 0)))


def _layer(x, xb, pe_b, W, cache_k, cache_v, gla_s0, gdn_s0, conv_s0, *, bp, sp, bs, ss):
    n_prompt = bp * sp
    win = _split_w_in(W['w_in'])
    ident = _ep_identity
    swa_q = _matmul(xb, win['swa_q'], [F32], ident, name="proj_swa_q")[0]
    swa_kv = _matmul(xb, win['swa_kv'], [F32], ident, tn=2 * SWA_KV_W, name="proj_swa_kv")[0]
    gla_qkv = _matmul(xb, win['gla_qkv'], [F32], ident, name="proj_gla_qkv")[0]
    slab = _matmul(xb, win['slab'], [F32], ident, tn=LANES, name="proj_slab")[0]
    gla_r = _matmul(xb, win['gla_r'], [F32], ident, name="proj_gla_r")[0]
    gdn_qkv = _matmul(xb, win['gdn_qkv'], [F32], ident, name="proj_gdn_qkv")[0]
    gdn_z = _matmul(xb, win['gdn_z'], [F32], ident, name="proj_gdn_z")[0]
    gates = _matmul(xb, win['gates'], [F32], ident, name="proj_gates")[0]

    bias, sink = _swa_tables(W['swa_sinks'])
    q4 = swa_q.reshape(-1, SWA_KV_HEADS, SWA_GROUP, SWA_HEAD_DIM)
    kp, vp = _kv_windows(swa_kv[:n_prompt], bp, sp)
    ks, vs = _kv_windows(swa_kv[n_prompt:], bs, ss)
    front = ((0, 0), (0, 0), (SWA_WINDOW, 0), (0, 0))
    ck = jnp.transpose(cache_k, (0, 2, 1, 3))
    cv = jnp.transpose(cache_v, (0, 2, 1, 3))
    ks_full = jnp.concatenate([ck, ks], axis=2)
    vs_full = jnp.concatenate([cv, vs], axis=2)
    o_a_p = _swa(q4, jnp.pad(kp, front).astype(BF16), jnp.pad(vp, front).astype(BF16), bias, sink,
                 batch=bp, n_chunks=sp // CHUNK, row_block0=0, mask_front=True)
    o_a_s = _swa(q4, ks_full.astype(BF16), vs_full.astype(BF16), bias, sink,
                 batch=bs, n_chunks=ss // CHUNK, row_block0=n_prompt // CHUNK, mask_front=False)
    o_a = jnp.concatenate([o_a_p, o_a_s], axis=0).reshape(-1, SWA_Q_W).astype(BF16)
    n_keep = min(SWA_WINDOW, sp)
    new_k_p = jnp.transpose(kp[:, :, sp - n_keep:], (0, 2, 1, 3))
    new_v_p = jnp.transpose(vp[:, :, sp - n_keep:], (0, 2, 1, 3))
    new_k_s = jnp.transpose(ks_full[:, :, ss:], (0, 2, 1, 3))
    new_v_s = jnp.transpose(vs_full[:, :, ss:], (0, 2, 1, 3))

    wg_pad = jnp.zeros((LANES, GLA_K_W), F32).at[SLAB_GLR:SLAB_GLR + GLA_RANK].set(W['gla_w_gate2'])
    gbias = W['gla_gate_bias'].reshape(1, GLA_K_W)
    gnorm = W['gla_norm_w'].reshape(1, GLA_HEAD_V)
    zeros_gla = jnp.zeros((bp, GLA_HEADS, GLA_HEAD_K, GLA_HEAD_V), F32)
    o_b_p, gla_p = _gla(gla_qkv, slab, gla_r, wg_pad, gbias, gnorm, zeros_gla,
                        batch=bp, n_chunks=sp // CHUNK, row_block0=0)
    o_b_s, gla_s = _gla(gla_qkv, slab, gla_r, wg_pad, gbias, gnorm, gla_s0,
                        batch=bs, n_chunks=ss // CHUNK, row_block0=n_prompt // CHUNK)
    o_b = jnp.concatenate([o_b_p, o_b_s], axis=0)

    alog_row = jnp.zeros((1, LANES), F32).at[0, SLAB_DECAY:SLAB_DECAY + GDN_HEADS].set(W['gdn_a_log'])
    dtb_row = jnp.zeros((1, LANES), F32).at[0, SLAB_DECAY:SLAB_DECAY + GDN_HEADS].set(W['gdn_dt_bias'])
    dnorm = W['gdn_norm_w'].reshape(1, GDN_HEAD_V)
    zeros_gdn = jnp.zeros((bp, GDN_HEADS, GDN_HEAD_K, GDN_HEAD_V), F32)
    zeros_conv = jnp.zeros((bp, CONV_PAD_ROWS, GDN_CONV_CH), F32)
    o_c_p, conv_p, gdn_p = _gdn(gdn_qkv, slab, gdn_z, W['gdn_conv_w'], alog_row, dtb_row, dnorm, zeros_conv, zeros_gdn,
                                batch=bp, n_chunks=sp // CHUNK, row_block0=0)
    o_c_s, conv_s, gdn_s = _gdn(gdn_qkv, slab, gdn_z, W['gdn_conv_w'], alog_row, dtb_row, dnorm,
                                _pad_rows(conv_s0, CONV_PAD_ROWS), gdn_s0,
                                batch=bs, n_chunks=ss // CHUNK, row_block0=n_prompt // CHUNK)
    o_c = jnp.concatenate([o_c_p, o_c_s], axis=0)
    keep = CONV_PAD_ROWS - (GDN_CONV - 1)

    w_br = jnp.stack([W['w_br_swa'], W['w_br_gla'], W['w_br_gdn']]).astype(BF16)
    merged = _merge(jnp.stack([o_a, o_b, o_c]), w_br, gates)
    tm = _row_tile(x.shape[0], 1024)
    tn_res = 512
    z = _matmul(merged, W['w_out'].astype(BF16), [F32], _ep_residual, extras=[(x, *_tile_spec(tm, tn_res))],
                tm=tm, tn=tn_res, name="out_proj")[0]
    x, xb = _layer_norm(z, W['ln1_g'], W['ln1_b'])
    hid = _matmul(xb, W['w_up'].astype(BF16), [BF16], _ep_relu_sq, name="ffn_up")[0]
    z = _matmul(hid, W['w_down'].astype(BF16), [F32], _ep_residual, extras=[(x, *_tile_spec(tm, 1024))],
                tm=tm, tk=2048, name="ffn_down")[0]
    x, xb = _layer_norm(z, W['ln2_g'], W['ln2_b'])
    pe_extras = [(x, *_tile_spec(tm, tn_res)),
                 (pe_b, (tm, PE_DIM), lambda i, j, k: (i, 0)),
                 (W['pe_w_proj'].astype(BF16), (PE_DIM, tn_res), lambda i, j, k: (0, j))]
    z = _matmul(xb, W['pe_w_gate'].astype(BF16), [F32], _ep_pe, extras=pe_extras, tm=tm, tn=tn_res,
                name="pe_gate")[0]
    x, xb = _layer_norm(z, W['ln3_g'], W['ln3_b'])
    states_p = (new_k_p, new_v_p, gla_p, gdn_p, conv_p[:, keep:])
    states_s = (new_k_s, new_v_s, gla_s, gdn_s, conv_s[:, keep:])
    return x, xb, states_p, states_s


def kernel(x_prompt, x_sample, cache_swa_k, cache_swa_v, state_gla, state_gdn, state_gdn_conv, p_prompt, p_sample, w_in, swa_sinks, gla_w_gate2, gla_gate_bias, gla_norm_w, gdn_conv_w, gdn_a_log, gdn_dt_bias, gdn_norm_w, w_br_swa, w_br_gla, w_br_gdn, w_out, ln1_g, ln1_b, w_up, w_down, ln2_g, ln2_b, pe_w_gate, pe_w_proj, ln3_g, ln3_b):
    bp, sp, d = x_prompt.shape
    bs, ss, _ = x_sample.shape
    n_prompt = bp * sp
    x = jnp.concatenate([x_prompt.reshape(n_prompt, d), x_sample.reshape(bs * ss, d)], axis=0)
    xb = x.astype(BF16)
    st_p, st_s = [], []
    for l in range(DEPTH):
        W = {'w_in': w_in[l], 'swa_sinks': swa_sinks[l], 'gla_w_gate2': gla_w_gate2[l],
             'gla_gate_bias': gla_gate_bias[l], 'gla_norm_w': gla_norm_w[l], 'gdn_conv_w': gdn_conv_w[l],
             'gdn_a_log': gdn_a_log[l], 'gdn_dt_bias': gdn_dt_bias[l], 'gdn_norm_w': gdn_norm_w[l],
             'w_br_swa': w_br_swa[l], 'w_br_gla': w_br_gla[l], 'w_br_gdn': w_br_gdn[l], 'w_out': w_out[l],
             'ln1_g': ln1_g[l], 'ln1_b': ln1_b[l], 'w_up': w_up[l], 'w_down': w_down[l],
             'ln2_g': ln2_g[l], 'ln2_b': ln2_b[l], 'pe_w_gate': pe_w_gate[l], 'pe_w_proj': pe_w_proj[l],
             'ln3_g': ln3_g[l], 'ln3_b': ln3_b[l]}
        pe_b = jnp.concatenate([p_prompt[l].reshape(n_prompt, PE_DIM), p_sample[l].reshape(bs * ss, PE_DIM)],
                               axis=0).astype(BF16)
        x, xb, sp_l, ss_l = _layer(x, xb, pe_b, W, cache_swa_k[l], cache_swa_v[l], state_gla[l], state_gdn[l],
                                   state_gdn_conv[l], bp=bp, sp=sp, bs=bs, ss=ss)
        st_p.append(sp_l)
        st_s.append(ss_l)
    y_prompt = x[:n_prompt].reshape(bp, sp, d)
    y_sample = x[n_prompt:].reshape(bs, ss, d)
    stack = lambda sts, i: jnp.stack([s[i] for s in sts])
    return (y_prompt, y_sample,
            stack(st_p, 0), stack(st_p, 1), stack(st_p, 2), stack(st_p, 3), stack(st_p, 4),
            stack(st_s, 0), stack(st_s, 1), stack(st_s, 2), stack(st_s, 3), stack(st_s, 4))
```

```python
import functools
from typing import NamedTuple

import jax
import jax.numpy as jnp
from jax import lax
from jax.experimental import pallas as pl
from jax.experimental.pallas import tpu as pltpu

F32 = jnp.float32
BF16 = jnp.bfloat16
HIGHEST = lax.Precision.HIGHEST

D_MODEL = 4096
DEPTH = 2
CHUNK = 64
PE_DIM = 256
D_FF = 4 * D_MODEL
LN_EPS = 1e-5
RMS_EPS = 1e-6
NEG_INF = -1e30

SWA_HEADS = 32
SWA_KV_HEADS = 4
SWA_GROUP = SWA_HEADS // SWA_KV_HEADS
SWA_HEAD_DIM = 64
SWA_WINDOW = 128
SWA_KEYS = SWA_WINDOW + CHUNK
GLA_HEADS = 4
GLA_HEAD_K = 256
GLA_HEAD_V = 512
GLA_RANK = 16
GLA_TAU = 16.0
GDN_HEADS = 16
GDN_HEAD_K = 128
GDN_HEAD_V = 128
GDN_CONV = 4

SWA_Q_W = SWA_HEADS * SWA_HEAD_DIM
SWA_KV_W = SWA_KV_HEADS * SWA_HEAD_DIM
GLA_K_W = GLA_HEADS * GLA_HEAD_K
GLA_V_W = GLA_HEADS * GLA_HEAD_V
GDN_K_W = GDN_HEADS * GDN_HEAD_K
GDN_V_W = GDN_HEADS * GDN_HEAD_V
GDN_CONV_CH = 2 * GDN_K_W + GDN_V_W
N_BRANCH = 3
IN_SIZES = (SWA_Q_W, SWA_KV_W, SWA_KV_W, GLA_K_W, GLA_K_W, GLA_V_W, GLA_RANK, GLA_V_W,
            GDN_CONV_CH, GDN_HEADS, GDN_HEADS, GDN_V_W, N_BRANCH * D_MODEL)

DN_ALPHA = (2 * DEPTH) ** 0.25

LANES = 128
SUBLANES = 8
VMEM_LIMIT_BYTES = 56 * 1024 * 1024
SLAB_GLR = 0
SLAB_BETA = 16
SLAB_DECAY = 32
CONV_PAD_ROWS = SUBLANES


def _params(*sem):
    return pltpu.CompilerParams(dimension_semantics=sem, vmem_limit_bytes=VMEM_LIMIT_BYTES)


def _row_tile(m, want):
    t = want
    while m % t:
        t //= 2
    assert t >= SUBLANES
    return t


class _Steps(NamedTuple):
    bp: int
    npc: int
    bs: int
    nsc: int

    @property
    def n_prompt(self):
        return self.bp * self.npc

    @property
    def n_steps(self):
        return self.bp * self.npc + self.bs * self.nsc

    def locate(self, s):
        is_prompt = s < self.n_prompt
        r = jnp.maximum(s - self.n_prompt, 0)
        seq = jnp.where(is_prompt, s // self.npc, self.bp + r // self.nsc)
        chunk = jnp.where(is_prompt, s % self.npc, r % self.nsc)
        return seq, chunk, is_prompt

    def sample_index(self, s):
        seq, _, _ = self.locate(s)
        return jnp.maximum(seq - self.bp, 0)


def _dot(a, b):
    return jnp.dot(a, b, preferred_element_type=F32)


def _dot_nt(a, b):
    return lax.dot_general(a, b, (((1,), (1,)), ((), ())), preferred_element_type=F32)


def _dot_tn(a, b):
    return lax.dot_general(a, b, (((0,), (0,)), ((), ())), preferred_element_type=F32)


def _sigmoid(x):
    return 0.5 * jnp.tanh(0.5 * x) + 0.5


def _silu(x):
    return x * _sigmoid(x)


def _softplus(x):
    return jnp.maximum(x, 0.0) + jnp.log(1.0 + jnp.exp(-jnp.abs(x)))


def _mm_body(*refs, nk, n_extra, n_out, epilogue):
    a_ref, w_ref = refs[0], refs[1]
    extra = refs[2:2 + n_extra]
    outs = refs[2 + n_extra:2 + n_extra + n_out]

    def finish(acc):
        vals = epilogue(acc, *extra)
        for o, v in zip(outs, vals):
            o[...] = v.astype(o.dtype)

    if nk == 1:
        finish(_dot(a_ref[...], w_ref[...]))
    else:
        acc_ref = refs[-1]
        k = pl.program_id(2)

        @pl.when(k == 0)
        def _():
            acc_ref[...] = jnp.zeros_like(acc_ref)

        acc_ref[...] += _dot(a_ref[...], w_ref[...])

        @pl.when(k == nk - 1)
        def _():
            finish(acc_ref[...])


def _matmul(a, w, out_dtypes, epilogue, extras=(), *, tm=1024, tn=1024, tk=None, name):
    M, K = a.shape
    N = w.shape[1]
    tm = _row_tile(M, tm)
    tn = min(tn, N)
    tk = K if tk is None else tk
    assert M % tm == 0 and N % tn == 0 and K % tk == 0
    nk = K // tk
    in_specs = [pl.BlockSpec((tm, tk), lambda i, j, k: (i, k)),
                pl.BlockSpec((tk, tn), lambda i, j, k: (k, j))]
    args = [a, w]
    for arr, block, imap in extras:
        in_specs.append(pl.BlockSpec(block, imap))
        args.append(arr)
    out_specs = [pl.BlockSpec((tm, tn), lambda i, j, k: (i, j)) for _ in out_dtypes]
    out_shape = [jax.ShapeDtypeStruct((M, N), dt) for dt in out_dtypes]
    scratch = [pltpu.VMEM((tm, tn), F32)] if nk > 1 else []
    body = functools.partial(_mm_body, nk=nk, n_extra=len(extras), n_out=len(out_dtypes), epilogue=epilogue)
    return pl.pallas_call(
        body, grid=(M // tm, N // tn, nk), in_specs=in_specs, out_specs=out_specs, out_shape=out_shape,
        scratch_shapes=scratch, compiler_params=_params("parallel", "parallel", "arbitrary"), name=name,
    )(*args)


def _tile_spec(tm, tn):
    return (tm, tn), (lambda i, j, k: (i, j))


def _ep_identity(acc):
    return (acc,)


def _ep_relu_sq(acc):
    r = jnp.maximum(acc, 0.0)
    return (r * r,)


def _ep_residual(acc, x_ref):
    return (DN_ALPHA * x_ref[...] + acc,)


def _ep_pe(acc, x_ref, pe_ref, wp_ref):
    proj = _dot(pe_ref[...], wp_ref[...])
    return (DN_ALPHA * x_ref[...] + _sigmoid(acc) * proj,)


def _merge_body(x_ref, wg_ref, o_ref, w_ref, out_ref, acc_ref):
    b = pl.program_id(2)
    gate = _sigmoid(_dot(x_ref[...], wg_ref[...]))
    contrib = gate * _dot(o_ref[...], w_ref[...])

    @pl.when(b == 0)
    def _():
        acc_ref[...] = contrib

    @pl.when(b > 0)
    def _():
        acc_ref[...] += contrib

    @pl.when(b == N_BRANCH - 1)
    def _():
        out_ref[...] = acc_ref[...].astype(out_ref.dtype)


def _merge(xb, w_gates, o_stack, w_stack, *, tm=1024, tn=512):
    _, M, K = o_stack.shape
    D = xb.shape[1]
    N = w_stack.shape[2]
    tm = _row_tile(M, tm)
    nj = N // tn
    return pl.pallas_call(
        _merge_body, grid=(M // tm, nj, N_BRANCH),
        in_specs=[pl.BlockSpec((tm, D), lambda i, j, b: (i, 0)),
                  pl.BlockSpec((D, tn), lambda i, j, b: (0, b * nj + j)),
                  pl.BlockSpec((None, tm, K), lambda i, j, b: (b, i, 0)),
                  pl.BlockSpec((None, K, tn), lambda i, j, b: (b, 0, j))],
        out_specs=pl.BlockSpec((tm, tn), lambda i, j, b: (i, j)),
        out_shape=jax.ShapeDtypeStruct((M, N), BF16),
        scratch_shapes=[pltpu.VMEM((tm, tn), F32)],
        compiler_params=_params("parallel", "parallel", "arbitrary"), name="merge_branches",
    )(xb, w_gates, o_stack, w_stack)


def _ln_rows(z, g, b):
    mu = jnp.mean(z, axis=-1, keepdims=True)
    d = z - mu
    var = jnp.mean(d * d, axis=-1, keepdims=True)
    return d * lax.rsqrt(var + LN_EPS) * g + b


def _ln_body(z_ref, g_ref, b_ref, x_ref, xb_ref):
    y = _ln_rows(z_ref[...], g_ref[...], b_ref[...])
    x_ref[...] = y
    xb_ref[...] = y.astype(BF16)


def _layer_norm(z, g, b, *, tm=256):
    M, D = z.shape
    tm = _row_tile(M, tm)
    row = pl.BlockSpec((tm, D), lambda i: (i, 0))
    vec = pl.BlockSpec((1, D), lambda i: (0, 0))
    return pl.pallas_call(
        _ln_body, grid=(M // tm,), in_specs=[row, vec, vec], out_specs=[row, row],
        out_shape=[jax.ShapeDtypeStruct((M, D), F32), jax.ShapeDtypeStruct((M, D), BF16)],
        compiler_params=_params("parallel"), name="layer_norm",
    )(z, g.reshape(1, D), b.reshape(1, D))


def _mm_ln_body(*refs, nj, nk, tn, n_extra, epilogue):
    a_ref, w_ref = refs[0], refs[1]
    extra = refs[2:2 + n_extra]
    g_ref, b_ref, x_ref, xb_ref = refs[2 + n_extra:6 + n_extra]
    j = pl.program_id(1)

    def finish(acc):
        z = epilogue(acc, *extra)[0]
        for jj in range(nj):
            @pl.when(j == jj)
            def _(jj=jj):
                x_ref[:, jj * tn:(jj + 1) * tn] = z

        @pl.when(j == nj - 1)
        def _():
            y = _ln_rows(x_ref[...], g_ref[...], b_ref[...])
            x_ref[...] = y
            xb_ref[...] = y.astype(BF16)

    if nk == 1:
        finish(_dot(a_ref[...], w_ref[...]))
    else:
        acc_ref = refs[-1]
        k = pl.program_id(2)

        @pl.when(k == 0)
        def _():
            acc_ref[...] = jnp.zeros_like(acc_ref)

        acc_ref[...] += _dot(a_ref[...], w_ref[...])

        @pl.when(k == nk - 1)
        def _():
            finish(acc_ref[...])


def _matmul_ln(a, w, g, b, epilogue, extras, *, tm=512, tn=512, tk=None, name):
    M, K = a.shape
    N = w.shape[1]
    tm = _row_tile(M, tm)
    tk = K if tk is None else tk
    assert N % tn == 0 and K % tk == 0
    nj, nk = N // tn, K // tk
    in_specs = [pl.BlockSpec((tm, tk), lambda i, j, k: (i, k)),
                pl.BlockSpec((tk, tn), lambda i, j, k: (k, j))]
    args = [a, w]
    for arr, block, imap in extras:
        in_specs.append(pl.BlockSpec(block, imap))
        args.append(arr)
    vec = pl.BlockSpec((1, N), lambda i, j, k: (0, 0))
    row = pl.BlockSpec((tm, N), lambda i, j, k: (i, 0))
    scratch = [pltpu.VMEM((tm, tn), F32)] if nk > 1 else []
    body = functools.partial(_mm_ln_body, nj=nj, nk=nk, tn=tn, n_extra=len(extras), epilogue=epilogue)
    return pl.pallas_call(
        body, grid=(M // tm, nj, nk), in_specs=in_specs + [vec, vec], out_specs=[row, row],
        out_shape=[jax.ShapeDtypeStruct((M, N), F32), jax.ShapeDtypeStruct((M, N), BF16)],
        scratch_shapes=scratch, compiler_params=_params("parallel", "arbitrary", "arbitrary"), name=name,
    )(*args, g.reshape(1, N), b.reshape(1, N))


def _swa_body(q_ref, k0_ref, k1_ref, k2_ref, v0_ref, v1_ref, v2_ref, bias_ref, sink_ref, o_ref, *, steps):
    _, chunk, is_prompt = steps.locate(pl.program_id(0))
    rows = CHUNK * SWA_GROUP
    J = range(SWA_KV_HEADS)
    q = [(q_ref[:, j] * (SWA_HEAD_DIM ** -0.5)).reshape(rows, SWA_HEAD_DIM).astype(BF16) for j in J]
    k = [jnp.concatenate([k0_ref[j], k1_ref[j], k2_ref[j]], axis=0) for j in J]
    v = [jnp.concatenate([v0_ref[j], v1_ref[j], v2_ref[j]], axis=0) for j in J]
    s = [_dot_nt(q[j], k[j]) - bias_ref[j] for j in J]
    col = lax.broadcasted_iota(jnp.int32, s[0].shape, 1)
    valid = col + jnp.where(is_prompt, chunk * CHUNK, SWA_WINDOW) >= SWA_WINDOW
    s = [jnp.where(valid, x, NEG_INF) for x in s]
    m = [jnp.maximum(jnp.max(s[j], axis=-1, keepdims=True), sink_ref[j]) for j in J]
    p = [jnp.exp(s[j] - m[j]) for j in J]
    denom = [jnp.sum(p[j], axis=-1, keepdims=True) + jnp.exp(sink_ref[j] - m[j]) for j in J]
    o = [_dot(p[j].astype(BF16), v[j]) / denom[j] for j in J]
    for j in J:
        o_ref[:, j] = o[j].reshape(CHUNK, SWA_GROUP, SWA_HEAD_DIM)


def _swa(q4, k_all, v_all, bias, sink, *, steps):
    back = SWA_WINDOW // CHUNK

    def first_key_chunk(s):
        seq, chunk, is_prompt = steps.locate(s)
        prompt_chunks = steps.bp * (steps.npc + back)
        return jnp.where(is_prompt, seq * (steps.npc + back), prompt_chunks + (seq - steps.bp) * (steps.nsc + back)) + chunk

    def win(w):
        return pl.BlockSpec((SWA_KV_HEADS, CHUNK, SWA_HEAD_DIM), lambda s: (0, first_key_chunk(s) + w, 0))

    rows = CHUNK * SWA_GROUP
    q_spec = pl.BlockSpec((CHUNK, SWA_KV_HEADS, SWA_GROUP, SWA_HEAD_DIM), lambda s: (s, 0, 0, 0))
    bias_spec = pl.BlockSpec((SWA_KV_HEADS, rows, SWA_KEYS), lambda s: (0, 0, 0))
    sink_spec = pl.BlockSpec((SWA_KV_HEADS, rows, 1), lambda s: (0, 0, 0))
    return pl.pallas_call(
        functools.partial(_swa_body, steps=steps), grid=(steps.n_steps,),
        in_specs=[q_spec, win(0), win(1), win(2), win(0), win(1), win(2), bias_spec, sink_spec],
        out_specs=q_spec, out_shape=jax.ShapeDtypeStruct(q4.shape, F32),
        compiler_params=_params("parallel"), name="swa_chunk",
    )(q4, k_all, k_all, k_all, v_all, v_all, v_all, bias, sink)


def _gla_body(q_ref, k_ref, v_ref, slab_ref, r_ref, wg_ref, gb_ref, nw_ref, s0_ref, o_ref, state_ref, *, steps):
    _, chunk, is_prompt = steps.locate(pl.program_id(0))

    @pl.when((chunk == 0) & is_prompt)
    def _():
        state_ref[...] = jnp.zeros_like(state_ref)

    @pl.when((chunk == 0) & jnp.logical_not(is_prompt))
    def _():
        state_ref[...] = s0_ref[...]

    row = lax.broadcasted_iota(jnp.int32, (CHUNK, CHUNK), 0)
    col = lax.broadcasted_iota(jnp.int32, (CHUNK, CHUNK), 1)
    causal = row >= col
    tril = causal.astype(F32)

    ga = jnp.dot(slab_ref[...], wg_ref[...], precision=HIGHEST, preferred_element_type=F32) + gb_ref[...]
    log_a = (jnp.minimum(ga, 0.0) - jnp.log(1.0 + jnp.exp(-jnp.abs(ga)))) * (1.0 / GLA_TAU)
    b = jnp.dot(tril, log_a, precision=HIGHEST, preferred_element_type=F32)
    b_last = b[CHUNK - 1:CHUNK, :]
    ones = jnp.ones((CHUNK, LANES), F32)
    b_last_col = lax.dot_general(log_a, ones, (((0,), (0,)), ((), ())), precision=HIGHEST,
                                 preferred_element_type=F32)[:, :1]

    qd_all = (q_ref[...] * (GLA_HEAD_K ** -0.5) * jnp.exp(b)).astype(BF16)
    k_all = k_ref[...]
    kd_all = (k_all * jnp.exp(-b)).astype(BF16)
    kt_all = (k_all * jnp.exp(b_last - b)).astype(BF16)
    scale_col = jnp.exp(b_last_col)

    H = range(GLA_HEADS)
    ksl = lambda t, h: t[:, h * GLA_HEAD_K:(h + 1) * GLA_HEAD_K]
    v = [v_ref[:, h * GLA_HEAD_V:(h + 1) * GLA_HEAD_V].astype(BF16) for h in H]
    state = [state_ref[h] for h in H]
    att = [jnp.where(causal, _dot_nt(ksl(qd_all, h), ksl(kd_all, h)), 0.0).astype(BF16) for h in H]
    o_state = [_dot(ksl(qd_all, h), state[h].astype(BF16)) for h in H]
    o = [_dot(att[h], v[h]) + o_state[h] for h in H]
    new_state = [state[h] * scale_col[h * GLA_HEAD_K:(h + 1) * GLA_HEAD_K] + _dot_tn(ksl(kt_all, h), v[h]) for h in H]
    for h in H:
        state_ref[h] = new_state[h]
    outs = []
    for h in H:
        ms = jnp.mean(o[h] * o[h], axis=-1, keepdims=True)
        gate = _silu(r_ref[:, h * GLA_HEAD_V:(h + 1) * GLA_HEAD_V])
        outs.append((o[h] * lax.rsqrt(ms + RMS_EPS) * nw_ref[...] * gate).astype(o_ref.dtype))
    o_ref[...] = jnp.concatenate(outs, axis=1)


def _gla(qkv, slab, gate_r, wg_pad, gate_bias, norm_w, s0, *, steps):
    v_block = 2 * GLA_K_W // GLA_V_W
    state_block = (None, GLA_HEADS, GLA_HEAD_K, GLA_HEAD_V)
    in_specs = [
        pl.BlockSpec((CHUNK, GLA_K_W), lambda s: (s, 0)),
        pl.BlockSpec((CHUNK, GLA_K_W), lambda s: (s, 1)),
        pl.BlockSpec((CHUNK, GLA_V_W), lambda s: (s, v_block)),
        pl.BlockSpec((CHUNK, LANES), lambda s: (s, 0)),
        pl.BlockSpec((CHUNK, GLA_V_W), lambda s: (s, 0)),
        pl.BlockSpec((LANES, GLA_K_W), lambda s: (0, 0)),
        pl.BlockSpec((1, GLA_K_W), lambda s: (0, 0)),
        pl.BlockSpec((1, GLA_HEAD_V), lambda s: (0, 0)),
        pl.BlockSpec(state_block, lambda s: (steps.sample_index(s), 0, 0, 0)),
    ]
    out_specs = [
        pl.BlockSpec((CHUNK, GLA_V_W), lambda s: (s, 0)),
        pl.BlockSpec(state_block, lambda s: (steps.locate(s)[0], 0, 0, 0)),
    ]
    out_shape = [jax.ShapeDtypeStruct((steps.n_steps * CHUNK, GLA_V_W), BF16),
                 jax.ShapeDtypeStruct((steps.bp + steps.bs, GLA_HEADS, GLA_HEAD_K, GLA_HEAD_V), F32)]
    return pl.pallas_call(
        functools.partial(_gla_body, steps=steps), grid=(steps.n_steps,), in_specs=in_specs, out_specs=out_specs,
        out_shape=out_shape, compiler_params=_params("arbitrary"), name="gla_chunk",
    )(qkv, qkv, qkv, slab, gate_r, wg_pad, gate_bias, norm_w, s0)


def _gdn_body(x_ref, slab_ref, z_ref, cw_ref, alog_ref, dtb_ref, nw_ref, conv0_ref, s0_ref,
              o_ref, conv_ref, state_ref, y_ref, *, steps):
    _, chunk, is_prompt = steps.locate(pl.program_id(0))

    @pl.when((chunk == 0) & is_prompt)
    def _():
        conv_ref[...] = jnp.zeros_like(conv_ref)
        state_ref[...] = jnp.zeros_like(state_ref)

    @pl.when((chunk == 0) & jnp.logical_not(is_prompt))
    def _():
        conv_ref[...] = conv0_ref[...]
        state_ref[...] = s0_ref[...]

    x = x_ref[...]
    full = jnp.concatenate([conv_ref[...], x], axis=0)
    conv = cw_ref[GDN_CONV - 1][None] * x
    for d in range(1, GDN_CONV):
        conv = conv + cw_ref[GDN_CONV - 1 - d][None] * full[CONV_PAD_ROWS - d:CONV_PAD_ROWS - d + CHUNK]
    conv_ref[...] = x[CHUNK - CONV_PAD_ROWS:]
    y = _silu(conv)
    n_qk = 2 * GDN_HEADS
    yqk = y[:, :n_qk, :]
    sq = (yqk * yqk).reshape(CHUNK * n_qk, GDN_HEAD_K).astype(BF16)
    ssq = _dot(sq, jnp.ones((GDN_HEAD_K, GDN_HEAD_K), BF16)).reshape(CHUNK, n_qk, GDN_HEAD_K)
    n_sl = y.shape[1]
    yn = jnp.concatenate([yqk * lax.rsqrt(ssq + 1e-6), y[:, n_qk:, :]], axis=1)
    y_ref[...] = yn.reshape(CHUNK * n_sl, GDN_HEAD_K)

    row = lax.broadcasted_iota(jnp.int32, (CHUNK, CHUNK), 0)
    col = lax.broadcasted_iota(jnp.int32, (CHUNK, CHUNK), 1)
    lower = row >= col
    strict = row > col
    eye = (row == col).astype(F32)
    tril = lower.astype(F32)

    slab = slab_ref[...]
    beta_all = _sigmoid(slab)
    lane = lax.broadcasted_iota(jnp.int32, slab.shape, 1)
    is_decay = (lane >= SLAB_DECAY) & (lane < SLAB_DECAY + GDN_HEADS)
    g_all = jnp.where(is_decay, -jnp.exp(alog_ref[...]) * _softplus(slab + dtb_ref[...]), 0.0)
    gc_all = jnp.dot(tril, g_all, precision=HIGHEST, preferred_element_type=F32)
    gc_rows = gc_all.T

    H = range(GDN_HEADS)
    def head_rows(s):
        return y_ref[pl.ds(s, CHUNK, stride=n_sl), :]

    qn = [head_rows(h) * (GDN_HEAD_K ** -0.5) for h in H]
    kn = [head_rows(GDN_HEADS + h) for h in H]
    vs = [head_rows(2 * GDN_HEADS + h) for h in H]
    gcc = [gc_all[:, SLAB_DECAY + h:SLAB_DECAY + h + 1] for h in H]
    gcr = [gc_rows[SLAB_DECAY + h:SLAB_DECAY + h + 1, :] for h in H]
    beta = [beta_all[:, SLAB_BETA + h:SLAB_BETA + h + 1] for h in H]
    egc = [jnp.exp(g) for g in gcc]
    g_last = [g[CHUNK - 1:CHUNK, :] for g in gcc]
    decay = [jnp.exp(jnp.where(lower, gcc[h] - gcr[h], -jnp.inf)) for h in H]
    kb = [k.astype(BF16) for k in kn]
    qb = [q.astype(BF16) for q in qn]
    state = [state_ref[h] for h in H]
    sb = [s.astype(BF16) for s in state]

    a = [_dot_nt(kb[h], kb[h]) * jnp.where(strict, decay[h], 0.0) * beta[h] for h in H]
    qk = [(_dot_nt(qb[h], kb[h]) * decay[h]).astype(BF16) for h in H]
    o_state = [_dot((qn[h] * egc[h]).astype(BF16), sb[h]) for h in H]
    t = [eye - a[h] for h in H]
    apb = [x.astype(BF16) for x in a]
    for _ in range(5):
        apb = [_dot(x, x).astype(BF16) for x in apb]
        t = [t[h] + _dot(t[h].astype(BF16), apb[h]) for h in H]
    rhs = [jnp.concatenate([vs[h] * beta[h], kn[h] * (beta[h] * egc[h])], axis=1).astype(BF16) for h in H]
    uw = [_dot(t[h].astype(BF16), rhs[h]) for h in H]
    v_new = [(uw[h][:, :GDN_HEAD_V] - _dot(uw[h][:, GDN_HEAD_V:].astype(BF16), sb[h])).astype(BF16) for h in H]
    o = [o_state[h] + _dot(qk[h], v_new[h]) for h in H]
    k_tail = [(kn[h] * jnp.exp(g_last[h] - gcc[h])).astype(BF16) for h in H]
    new_state = [state[h] * jnp.exp(g_last[h]) + _dot_tn(k_tail[h], v_new[h]) for h in H]
    for h in H:
        state_ref[h] = new_state[h]
    outs = []
    for h in H:
        ms = jnp.mean(o[h] * o[h], axis=-1, keepdims=True)
        zh = z_ref[:, h * GDN_HEAD_V:(h + 1) * GDN_HEAD_V]
        outs.append((o[h] * lax.rsqrt(ms + RMS_EPS) * nw_ref[...] * _silu(zh)).astype(o_ref.dtype))
    o_ref[...] = jnp.concatenate(outs, axis=1)


def _gdn(qkv, slab, z, conv_w, alog_row, dtb_row, norm_w, conv0, s0, *, steps):
    n_sl = GDN_CONV_CH // LANES
    n_seq = steps.bp + steps.bs
    conv_block = (None, CONV_PAD_ROWS, n_sl, LANES)
    state_block = (None, GDN_HEADS, GDN_HEAD_K, GDN_HEAD_V)
    seq_of = lambda s: steps.locate(s)[0]
    in_specs = [
        pl.BlockSpec((CHUNK, n_sl, LANES), lambda s: (s, 0, 0)),
        pl.BlockSpec((CHUNK, LANES), lambda s: (s, 0)),
        pl.BlockSpec((CHUNK, GDN_V_W), lambda s: (s, 0)),
        pl.BlockSpec((GDN_CONV, n_sl, LANES), lambda s: (0, 0, 0)),
        pl.BlockSpec((1, LANES), lambda s: (0, 0)),
        pl.BlockSpec((1, LANES), lambda s: (0, 0)),
        pl.BlockSpec((1, GDN_HEAD_V), lambda s: (0, 0)),
        pl.BlockSpec(conv_block, lambda s: (steps.sample_index(s), 0, 0, 0)),
        pl.BlockSpec(state_block, lambda s: (steps.sample_index(s), 0, 0, 0)),
    ]
    out_specs = [
        pl.BlockSpec((CHUNK, GDN_V_W), lambda s: (s, 0)),
        pl.BlockSpec(conv_block, lambda s: (seq_of(s), 0, 0, 0)),
        pl.BlockSpec(state_block, lambda s: (seq_of(s), 0, 0, 0)),
    ]
    out_shape = [jax.ShapeDtypeStruct((steps.n_steps * CHUNK, GDN_V_W), BF16),
                 jax.ShapeDtypeStruct((n_seq, CONV_PAD_ROWS, n_sl, LANES), F32),
                 jax.ShapeDtypeStruct((n_seq, GDN_HEADS, GDN_HEAD_K, GDN_HEAD_V), F32)]
    o, conv, state = pl.pallas_call(
        functools.partial(_gdn_body, steps=steps), grid=(steps.n_steps,), in_specs=in_specs, out_specs=out_specs,
        out_shape=out_shape, scratch_shapes=[pltpu.VMEM((CHUNK * n_sl, LANES), F32)],
        compiler_params=_params("arbitrary"), name="gdn_chunk",
    )(qkv.reshape(-1, n_sl, LANES), slab, z, conv_w.reshape(GDN_CONV, n_sl, LANES), alog_row, dtb_row, norm_w,
      conv0.reshape(steps.bs, CONV_PAD_ROWS, n_sl, LANES), s0)
    return o, conv.reshape(n_seq, CONV_PAD_ROWS, GDN_CONV_CH), state


def _split_w_in(w):
    edges = [0]
    for s in IN_SIZES:
        edges.append(edges[-1] + s)
    col = lambda a, b: w[:, edges[a]:edges[b]].astype(BF16)
    slab = jnp.zeros((D_MODEL, LANES), BF16)
    slab = slab.at[:, SLAB_GLR:SLAB_GLR + GLA_RANK].set(col(6, 7))
    slab = slab.at[:, SLAB_BETA:SLAB_BETA + GDN_HEADS].set(col(9, 10))
    slab = slab.at[:, SLAB_DECAY:SLAB_DECAY + GDN_HEADS].set(col(10, 11))
    return dict(swa_q=col(0, 1), swa_kv=col(1, 3), gla_qkv=col(3, 6), slab=slab, gla_r=col(7, 8),
                gdn_qkv=col(8, 9), gdn_z=col(11, 12), gates=col(12, 13))


def _swa_tables(sinks):
    rows = CHUNK * SWA_GROUP
    t = jnp.arange(rows) // SWA_GROUP
    g = jnp.arange(rows) % SWA_GROUP
    dist = jnp.abs((t + SWA_WINDOW)[:, None] - jnp.arange(SWA_KEYS)[None, :]).astype(F32)
    head = jnp.arange(SWA_KV_HEADS)[:, None] * SWA_GROUP + g[None, :]
    slopes = 2.0 ** (-8.0 * (head + 1).astype(F32) / SWA_HEADS)
    bias = slopes[:, :, None] * dist[None]
    sink = sinks.astype(F32)[head][:, :, None]
    return bias, sink


def _key_rows(new, cache, *, batch, seq):
    t = new.reshape(batch, seq, SWA_KV_HEADS, SWA_HEAD_DIM)
    if cache is None:
        return jnp.pad(t, ((0, 0), (SWA_WINDOW, 0), (0, 0), (0, 0)))
    return jnp.concatenate([cache, t], axis=1)


def _pad_rows(t, n_rows):
    return jnp.pad(t, ((0, 0), (n_rows - t.shape[1], 0), (0, 0)))


def _layer(x, xb, pe_b, W, cache_k, cache_v, gla_s0, gdn_s0, conv_s0, *, bp, sp, bs, ss):
    n_prompt = bp * sp
    steps = _Steps(bp=bp, npc=sp // CHUNK, bs=bs, nsc=ss // CHUNK)
    win = _split_w_in(W['w_in'])
    ident = _ep_identity
    swa_q = _matmul(xb, win['swa_q'], [F32], ident, name="proj_swa_q")[0]
    swa_kv = _matmul(xb, win['swa_kv'], [F32], ident, tn=2 * SWA_KV_W, name="proj_swa_kv")[0]
    gla_qkv = _matmul(xb, win['gla_qkv'], [F32], ident, name="proj_gla_qkv")[0]
    slab = _matmul(xb, win['slab'], [F32], ident, tn=LANES, name="proj_slab")[0]
    gla_r = _matmul(xb, win['gla_r'], [F32], ident, name="proj_gla_r")[0]
    gdn_qkv = _matmul(xb, win['gdn_qkv'], [F32], ident, name="proj_gdn_qkv")[0]
    gdn_z = _matmul(xb, win['gdn_z'], [F32], ident, name="proj_gdn_z")[0]

    bias, sink = _swa_tables(W['swa_sinks'])
    q4 = swa_q.reshape(-1, SWA_KV_HEADS, SWA_GROUP, SWA_HEAD_DIM)
    new_kv, kv_all = [], []
    for col, cache in ((0, cache_k), (1, cache_v)):
        t = swa_kv[:, col * SWA_KV_W:(col + 1) * SWA_KV_W]
        rows_p = _key_rows(t[:n_prompt], None, batch=bp, seq=sp)
        rows_s = _key_rows(t[n_prompt:], cache, batch=bs, seq=ss)
        new_kv.append((rows_p[:, sp:], rows_s[:, ss:]))
        flat = jnp.concatenate([rows_p.reshape(-1, SWA_KV_HEADS, SWA_HEAD_DIM),
                                rows_s.reshape(-1, SWA_KV_HEADS, SWA_HEAD_DIM)], axis=0)
        kv_all.append(jnp.transpose(flat, (1, 0, 2)).astype(BF16))
    o_a = _swa(q4, kv_all[0], kv_all[1], bias, sink, steps=steps).reshape(-1, SWA_Q_W).astype(BF16)

    wg_pad = jnp.zeros((LANES, GLA_K_W), F32).at[SLAB_GLR:SLAB_GLR + GLA_RANK].set(W['gla_w_gate2'])
    gbias = W['gla_gate_bias'].reshape(1, GLA_K_W)
    gnorm = W['gla_norm_w'].reshape(1, GLA_HEAD_V)
    o_b, gla_state = _gla(gla_qkv, slab, gla_r, wg_pad, gbias, gnorm, gla_s0, steps=steps)

    alog_row = jnp.zeros((1, LANES), F32).at[0, SLAB_DECAY:SLAB_DECAY + GDN_HEADS].set(W['gdn_a_log'])
    dtb_row = jnp.zeros((1, LANES), F32).at[0, SLAB_DECAY:SLAB_DECAY + GDN_HEADS].set(W['gdn_dt_bias'])
    dnorm = W['gdn_norm_w'].reshape(1, GDN_HEAD_V)
    o_c, conv_state, gdn_state = _gdn(gdn_qkv, slab, gdn_z, W['gdn_conv_w'], alog_row, dtb_row, dnorm,
                                      _pad_rows(conv_s0, CONV_PAD_ROWS), gdn_s0, steps=steps)
    conv_state = conv_state[:, CONV_PAD_ROWS - (GDN_CONV - 1):]

    w_br = jnp.stack([W['w_br_swa'], W['w_br_gla'], W['w_br_gdn']]).astype(BF16)
    merged = _merge(xb, win['gates'], jnp.stack([o_a, o_b, o_c]), w_br)
    tm = _row_tile(x.shape[0], 512)
    tn = 512
    x, xb = _matmul_ln(merged, W['w_out'].astype(BF16), W['ln1_g'], W['ln1_b'], _ep_residual,
                       [(x, *_tile_spec(tm, tn))], tm=tm, tn=tn, name="out_proj_ln")
    hid = _matmul(xb, W['w_up'].astype(BF16), [BF16], _ep_relu_sq, name="ffn_up")[0]
    tm_down = _row_tile(x.shape[0], 1024)
    z = _matmul(hid, W['w_down'].astype(BF16), [F32], _ep_residual, extras=[(x, *_tile_spec(tm_down, 1024))],
                tm=tm_down, tk=2048, name="ffn_down")[0]
    x, xb = _layer_norm(z, W['ln2_g'], W['ln2_b'])
    pe_extras = [(x, *_tile_spec(tm, tn)),
                 (pe_b, (tm, PE_DIM), lambda i, j, k: (i, 0)),
                 (W['pe_w_proj'].astype(BF16), (PE_DIM, tn), lambda i, j, k: (0, j))]
    x, xb = _matmul_ln(xb, W['pe_w_gate'].astype(BF16), W['ln3_g'], W['ln3_b'], _ep_pe, pe_extras,
                       tm=tm, tn=tn, name="pe_gate_ln")
    states_p = (new_kv[0][0], new_kv[1][0], gla_state[:bp], gdn_state[:bp], conv_state[:bp])
    states_s = (new_kv[0][1], new_kv[1][1], gla_state[bp:], gdn_state[bp:], conv_state[bp:])
    return x, xb, states_p, states_s


def kernel(x_prompt, x_sample, cache_swa_k, cache_swa_v, state_gla, state_gdn, state_gdn_conv, p_prompt, p_sample, w_in, swa_sinks, gla_w_gate2, gla_gate_bias, gla_norm_w, gdn_conv_w, gdn_a_log, gdn_dt_bias, gdn_norm_w, w_br_swa, w_br_gla, w_br_gdn, w_out, ln1_g, ln1_b, w_up, w_down, ln2_g, ln2_b, pe_w_gate, pe_w_proj, ln3_g, ln3_b):
    bp, sp, d = x_prompt.shape
    bs, ss, _ = x_sample.shape
    n_prompt = bp * sp
    x = jnp.concatenate([x_prompt.reshape(n_prompt, d), x_sample.reshape(bs * ss, d)], axis=0)
    xb = x.astype(BF16)
    st_p, st_s = [], []
    for l in range(DEPTH):
        W = {'w_in': w_in[l], 'swa_sinks': swa_sinks[l], 'gla_w_gate2': gla_w_gate2[l],
             'gla_gate_bias': gla_gate_bias[l], 'gla_norm_w': gla_norm_w[l], 'gdn_conv_w': gdn_conv_w[l],
             'gdn_a_log': gdn_a_log[l], 'gdn_dt_bias': gdn_dt_bias[l], 'gdn_norm_w': gdn_norm_w[l],
             'w_br_swa': w_br_swa[l], 'w_br_gla': w_br_gla[l], 'w_br_gdn': w_br_gdn[l], 'w_out': w_out[l],
             'ln1_g': ln1_g[l], 'ln1_b': ln1_b[l], 'w_up': w_up[l], 'w_down': w_down[l],
             'ln2_g': ln2_g[l], 'ln2_b': ln2_b[l], 'pe_w_gate': pe_w_gate[l], 'pe_w_proj': pe_w_proj[l],
             'ln3_g': ln3_g[l], 'ln3_b': ln3_b[l]}
        pe_b = jnp.concatenate([p_prompt[l].reshape(n_prompt, PE_DIM), p_sample[l].reshape(bs * ss, PE_DIM)],
                               axis=0).astype(BF16)
        x, xb, sp_l, ss_l = _layer(x, xb, pe_b, W, cache_swa_k[l], cache_swa_v[l], state_gla[l], state_gdn[l],
                                   state_gdn_conv[l], bp=bp, sp=sp, bs=bs, ss=ss)
        st_p.append(sp_l)
        st_s.append(ss_l)
    y_prompt = x[:n_prompt].reshape(bp, sp, d)
    y_sample = x[n_prompt:].reshape(bs, ss, d)
    stack = lambda sts, i: jnp.stack([s[i] for s in sts])
    return (y_prompt, y_sample,
            stack(st_p, 0), stack(st_p, 1), stack(st_p, 2), stack(st_p, 3), stack(st_p, 4),
            stack(st_s, 0), stack(st_s, 1), stack(st_s, 2), stack(st_s, 3), stack(st_s, 4))
```

```python
import functools
from typing import NamedTuple

import jax
import jax.numpy as jnp
from jax import lax
from jax.experimental import pallas as pl
from jax.experimental.pallas import tpu as pltpu

F32 = jnp.float32
BF16 = jnp.bfloat16
HIGHEST = lax.Precision.HIGHEST

D_MODEL = 4096
DEPTH = 2
CHUNK = 64
PE_DIM = 256
D_FF = 4 * D_MODEL
LN_EPS = 1e-5
RMS_EPS = 1e-6
NEG_INF = -1e30

SWA_HEADS = 32
SWA_KV_HEADS = 4
SWA_GROUP = SWA_HEADS // SWA_KV_HEADS
SWA_HEAD_DIM = 64
SWA_WINDOW = 128
SWA_KEYS = SWA_WINDOW + CHUNK
GLA_HEADS = 4
GLA_HEAD_K = 256
GLA_HEAD_V = 512
GLA_RANK = 16
GLA_TAU = 16.0
GDN_HEADS = 16
GDN_HEAD_K = 128
GDN_HEAD_V = 128
GDN_CONV = 4

SWA_Q_W = SWA_HEADS * SWA_HEAD_DIM
SWA_KV_W = SWA_KV_HEADS * SWA_HEAD_DIM
GLA_K_W = GLA_HEADS * GLA_HEAD_K
GLA_V_W = GLA_HEADS * GLA_HEAD_V
GDN_K_W = GDN_HEADS * GDN_HEAD_K
GDN_V_W = GDN_HEADS * GDN_HEAD_V
GDN_CONV_CH = 2 * GDN_K_W + GDN_V_W
N_BRANCH = 3
IN_SIZES = (SWA_Q_W, SWA_KV_W, SWA_KV_W, GLA_K_W, GLA_K_W, GLA_V_W, GLA_RANK, GLA_V_W,
            GDN_CONV_CH, GDN_HEADS, GDN_HEADS, GDN_V_W, N_BRANCH * D_MODEL)

DN_ALPHA = (2 * DEPTH) ** 0.25

LANES = 128
SUBLANES = 8
VMEM_LIMIT_BYTES = 56 * 1024 * 1024
SLAB_GLR = 0
SLAB_BETA = 16
SLAB_DECAY = 32
CONV_PAD_ROWS = SUBLANES
SWA_PAIRS = SWA_GROUP * SWA_HEAD_DIM // LANES
SWA_SEG = 256
SWA_PAD_BIAS = 1e30


def _params(*sem):
    return pltpu.CompilerParams(dimension_semantics=sem, vmem_limit_bytes=VMEM_LIMIT_BYTES)


def _row_tile(m, want):
    t = want
    while m % t:
        t //= 2
    assert t >= SUBLANES
    return t


class _Steps(NamedTuple):
    bp: int
    npc: int
    bs: int
    nsc: int

    @property
    def n_prompt(self):
        return self.bp * self.npc

    @property
    def n_steps(self):
        return self.bp * self.npc + self.bs * self.nsc

    def locate(self, s):
        is_prompt = s < self.n_prompt
        r = jnp.maximum(s - self.n_prompt, 0)
        seq = jnp.where(is_prompt, s // self.npc, self.bp + r // self.nsc)
        chunk = jnp.where(is_prompt, s % self.npc, r % self.nsc)
        return seq, chunk, is_prompt

    def sample_index(self, s):
        seq, _, _ = self.locate(s)
        return jnp.maximum(seq - self.bp, 0)


def _dot(a, b):
    return jnp.dot(a, b, preferred_element_type=F32)


def _dot_nt(a, b):
    return lax.dot_general(a, b, (((1,), (1,)), ((), ())), preferred_element_type=F32)


def _dot_tn(a, b):
    return lax.dot_general(a, b, (((0,), (0,)), ((), ())), preferred_element_type=F32)


def _sigmoid(x):
    return 0.5 * jnp.tanh(0.5 * x) + 0.5


def _silu(x):
    return x * _sigmoid(x)


def _softplus(x):
    return jnp.maximum(x, 0.0) + jnp.log(1.0 + jnp.exp(-jnp.abs(x)))


def _mm_body(*refs, nk, n_extra, n_out, epilogue):
    a_ref, w_ref = refs[0], refs[1]
    extra = refs[2:2 + n_extra]
    outs = refs[2 + n_extra:2 + n_extra + n_out]

    def finish(acc):
        vals = epilogue(acc, *extra)
        for o, v in zip(outs, vals):
            o[...] = v.astype(o.dtype)

    if nk == 1:
        finish(_dot(a_ref[...], w_ref[...]))
    else:
        acc_ref = refs[-1]
        k = pl.program_id(2)

        @pl.when(k == 0)
        def _():
            acc_ref[...] = jnp.zeros_like(acc_ref)

        acc_ref[...] += _dot(a_ref[...], w_ref[...])

        @pl.when(k == nk - 1)
        def _():
            finish(acc_ref[...])


def _matmul(a, w, layer, out_dtypes, epilogue, extras=(), *, tm=1024, tn=1024, tk=None, name):
    M, K = a.shape
    N = w.shape[2]
    tm = _row_tile(M, tm)
    tn = min(tn, N)
    tk = K if tk is None else tk
    assert M % tm == 0 and N % tn == 0 and K % tk == 0
    nk = K // tk
    in_specs = [pl.BlockSpec((tm, tk), lambda i, j, k: (i, k)),
                pl.BlockSpec((None, tk, tn), lambda i, j, k: (layer, k, j))]
    args = [a, w]
    for arr, block, imap in extras:
        in_specs.append(pl.BlockSpec(block, imap))
        args.append(arr)
    out_specs = [pl.BlockSpec((tm, tn), lambda i, j, k: (i, j)) for _ in out_dtypes]
    out_shape = [jax.ShapeDtypeStruct((M, N), dt) for dt in out_dtypes]
    scratch = [pltpu.VMEM((tm, tn), F32)] if nk > 1 else []
    body = functools.partial(_mm_body, nk=nk, n_extra=len(extras), n_out=len(out_dtypes), epilogue=epilogue)
    return pl.pallas_call(
        body, grid=(M // tm, N // tn, nk), in_specs=in_specs, out_specs=out_specs, out_shape=out_shape,
        scratch_shapes=scratch, compiler_params=_params("parallel", "parallel", "arbitrary"), name=name,
    )(*args)


def _tile_spec(tm, tn):
    return (tm, tn), (lambda i, j, k: (i, j))


def _ep_identity(acc):
    return (acc,)


def _ep_relu_sq(acc):
    r = jnp.maximum(acc, 0.0)
    return (r * r,)


def _ep_residual(acc, x_ref):
    return (DN_ALPHA * x_ref[...] + acc,)


def _ep_pe(acc, x_ref, pe_ref, wp_ref):
    proj = _dot(pe_ref[...], wp_ref[...])
    return (DN_ALPHA * x_ref[...] + _sigmoid(acc) * proj,)


def _merge_body(x_ref, wg_ref, o_ref, w_ref, out_ref, acc_ref):
    b = pl.program_id(2)
    gate = _sigmoid(_dot(x_ref[...], wg_ref[...]))
    contrib = gate * _dot(o_ref[...], w_ref[...])

    @pl.when(b == 0)
    def _():
        acc_ref[...] = contrib

    @pl.when(b > 0)
    def _():
        acc_ref[...] += contrib

    @pl.when(b == N_BRANCH - 1)
    def _():
        out_ref[...] = acc_ref[...].astype(out_ref.dtype)


def _merge(xb, w_gates, o_stack, w_stack, layer, *, tm=1024, tn=512):
    _, M, K = o_stack.shape
    D = xb.shape[1]
    N = w_stack.shape[3]
    tm = _row_tile(M, tm)
    nj = N // tn
    return pl.pallas_call(
        _merge_body, grid=(M // tm, nj, N_BRANCH),
        in_specs=[pl.BlockSpec((tm, D), lambda i, j, b: (i, 0)),
                  pl.BlockSpec((None, D, tn), lambda i, j, b: (layer, 0, b * nj + j)),
                  pl.BlockSpec((None, tm, K), lambda i, j, b: (b, i, 0)),
                  pl.BlockSpec((None, None, K, tn), lambda i, j, b: (layer, b, 0, j))],
        out_specs=pl.BlockSpec((tm, tn), lambda i, j, b: (i, j)),
        out_shape=jax.ShapeDtypeStruct((M, N), BF16),
        scratch_shapes=[pltpu.VMEM((tm, tn), F32)],
        compiler_params=_params("parallel", "parallel", "arbitrary"), name="merge_branches",
    )(xb, w_gates, o_stack, w_stack)


def _ln_rows(z, g, b):
    mu = jnp.mean(z, axis=-1, keepdims=True)
    d = z - mu
    var = jnp.mean(d * d, axis=-1, keepdims=True)
    return d * lax.rsqrt(var + LN_EPS) * g + b


def _ln_body(z_ref, g_ref, b_ref, x_ref, xb_ref):
    y = _ln_rows(z_ref[...], g_ref[...], b_ref[...])
    x_ref[...] = y
    xb_ref[...] = y.astype(BF16)


def _layer_norm(z, g, b, *, tm=256):
    M, D = z.shape
    tm = _row_tile(M, tm)
    row = pl.BlockSpec((tm, D), lambda i: (i, 0))
    vec = pl.BlockSpec((1, D), lambda i: (0, 0))
    return pl.pallas_call(
        _ln_body, grid=(M // tm,), in_specs=[row, vec, vec], out_specs=[row, row],
        out_shape=[jax.ShapeDtypeStruct((M, D), F32), jax.ShapeDtypeStruct((M, D), BF16)],
        compiler_params=_params("parallel"), name="layer_norm",
    )(z, g.reshape(1, D), b.reshape(1, D))


def _mm_ln_body(*refs, nj, nk, tn, n_extra, epilogue):
    a_ref, w_ref = refs[0], refs[1]
    extra = refs[2:2 + n_extra]
    g_ref, b_ref, x_ref, xb_ref = refs[2 + n_extra:6 + n_extra]
    j = pl.program_id(1)

    def finish(acc):
        z = epilogue(acc, *extra)[0]
        for jj in range(nj):
            @pl.when(j == jj)
            def _(jj=jj):
                x_ref[:, jj * tn:(jj + 1) * tn] = z

        @pl.when(j == nj - 1)
        def _():
            y = _ln_rows(x_ref[...], g_ref[...], b_ref[...])
            x_ref[...] = y
            xb_ref[...] = y.astype(BF16)

    if nk == 1:
        finish(_dot(a_ref[...], w_ref[...]))
    else:
        acc_ref = refs[-1]
        k = pl.program_id(2)

        @pl.when(k == 0)
        def _():
            acc_ref[...] = jnp.zeros_like(acc_ref)

        acc_ref[...] += _dot(a_ref[...], w_ref[...])

        @pl.when(k == nk - 1)
        def _():
            finish(acc_ref[...])


def _matmul_ln(a, w, layer, g, b, epilogue, extras, *, tm=512, tn=512, tk=None, rows=None, name):
    M, K = a.shape
    N = w.shape[2]
    row0, n_rows = (0, M) if rows is None else rows
    tm = _row_tile(n_rows, tm)
    tk = K if tk is None else tk
    assert N % tn == 0 and K % tk == 0 and row0 % tm == 0
    nj, nk = N // tn, K // tk
    rb0 = row0 // tm
    in_specs = [pl.BlockSpec((tm, tk), lambda i, j, k: (rb0 + i, k)),
                pl.BlockSpec((None, tk, tn), lambda i, j, k: (layer, k, j))]
    args = [a, w]
    for arr, block, imap in extras:
        in_specs.append(pl.BlockSpec(block, lambda i, j, k, imap=imap: imap(rb0 + i, j, k)))
        args.append(arr)
    vec = pl.BlockSpec((1, N), lambda i, j, k: (0, 0))
    row = pl.BlockSpec((tm, N), lambda i, j, k: (i, 0))
    scratch = [pltpu.VMEM((tm, tn), F32)] if nk > 1 else []
    body = functools.partial(_mm_ln_body, nj=nj, nk=nk, tn=tn, n_extra=len(extras), epilogue=epilogue)
    return pl.pallas_call(
        body, grid=(n_rows // tm, nj, nk), in_specs=in_specs + [vec, vec], out_specs=[row, row],
        out_shape=[jax.ShapeDtypeStruct((n_rows, N), F32), jax.ShapeDtypeStruct((n_rows, N), BF16)],
        scratch_shapes=scratch, compiler_params=_params("parallel", "arbitrary", "arbitrary"), name=name,
    )(*args, g.reshape(1, N), b.reshape(1, N))


def _swa_body(q_ref, k0_ref, k1_ref, k2_ref, v0_ref, v1_ref, v2_ref, bias_ref, sink_ref, o_ref, *, steps):
    _, chunk, is_prompt = steps.locate(pl.program_id(0))
    J = range(SWA_KV_HEADS)
    P = range(SWA_PAIRS)
    low = lax.broadcasted_iota(jnp.int32, (SWA_KEYS, LANES), 1) < SWA_HEAD_DIM
    zero = jnp.zeros((SWA_KEYS, LANES), BF16)
    seg_pad = jnp.zeros((SWA_SEG - SWA_KEYS, LANES), BF16)

    def pair_operand(t):
        return jnp.concatenate([jnp.where(low, t, zero), seg_pad, jnp.where(low, zero, t), seg_pad], axis=0)

    q = [(jnp.concatenate([q_ref[:, (j * SWA_PAIRS + p) * LANES:(j * SWA_PAIRS + p + 1) * LANES] for p in P], axis=0)
          * (SWA_HEAD_DIM ** -0.5)).astype(BF16) for j in J]
    k = [pair_operand(jnp.concatenate([k0_ref[j], k1_ref[j], k2_ref[j]], axis=0)) for j in J]
    v = [pair_operand(jnp.concatenate([v0_ref[j], v1_ref[j], v2_ref[j]], axis=0)) for j in J]
    s = [_dot_nt(q[j], k[j]) - bias_ref[j] for j in J]
    key = lax.broadcasted_iota(jnp.int32, s[0].shape, 1) & (SWA_SEG - 1)
    valid = key + jnp.where(is_prompt, chunk * CHUNK, SWA_WINDOW) >= SWA_WINDOW
    s = [jnp.where(valid, x, NEG_INF) for x in s]
    halves = [(x[:, :SWA_SEG], x[:, SWA_SEG:]) for x in s]
    sinks = [(sink_ref[j][:, 0:1], sink_ref[j][:, SWA_HEAD_DIM:SWA_HEAD_DIM + 1]) for j in J]
    m = [[jnp.maximum(jnp.max(halves[j][e], axis=-1, keepdims=True), sinks[j][e]) for e in (0, 1)] for j in J]
    p = [[jnp.exp(halves[j][e] - m[j][e]) for e in (0, 1)] for j in J]
    denom = [[jnp.sum(p[j][e], axis=-1, keepdims=True) + jnp.exp(sinks[j][e] - m[j][e]) for e in (0, 1)] for j in J]
    o = [_dot(jnp.concatenate(p[j], axis=1).astype(BF16), v[j]) for j in J]
    even = lax.broadcasted_iota(jnp.int32, o[0].shape, 1) < SWA_HEAD_DIM
    o = [o[j] / jnp.where(even, denom[j][0], denom[j][1]) for j in J]
    o_ref[...] = jnp.concatenate([o[j][p * CHUNK:(p + 1) * CHUNK] for j in J for p in P], axis=1).astype(o_ref.dtype)


def _swa(q, k_all, v_all, bias, sink, *, steps):
    back = SWA_WINDOW // CHUNK

    def first_key_chunk(s):
        seq, chunk, is_prompt = steps.locate(s)
        prompt_chunks = steps.bp * (steps.npc + back)
        return jnp.where(is_prompt, seq * (steps.npc + back), prompt_chunks + (seq - steps.bp) * (steps.nsc + back)) + chunk

    def win(w):
        return pl.BlockSpec((SWA_KV_HEADS, CHUNK, LANES), lambda s: (0, first_key_chunk(s) + w, 0))

    rows = CHUNK * SWA_PAIRS
    q_spec = pl.BlockSpec((CHUNK, SWA_Q_W), lambda s: (s, 0))
    bias_spec = pl.BlockSpec((SWA_KV_HEADS, rows, 2 * SWA_SEG), lambda s: (0, 0, 0))
    sink_spec = pl.BlockSpec((SWA_KV_HEADS, rows, LANES), lambda s: (0, 0, 0))
    return pl.pallas_call(
        functools.partial(_swa_body, steps=steps), grid=(steps.n_steps,),
        in_specs=[q_spec, win(0), win(1), win(2), win(0), win(1), win(2), bias_spec, sink_spec],
        out_specs=q_spec, out_shape=jax.ShapeDtypeStruct(q.shape, BF16),
        compiler_params=_params("parallel"), name="swa_chunk",
    )(q, k_all, k_all, k_all, v_all, v_all, v_all, bias, sink)


def _gla_body(q_ref, k_ref, v_ref, slab_ref, r_ref, wg_ref, gb_ref, nw_ref, s0_ref, o_ref, state_ref, *, steps):
    _, chunk, is_prompt = steps.locate(pl.program_id(0))

    @pl.when((chunk == 0) & is_prompt)
    def _():
        state_ref[...] = jnp.zeros_like(state_ref)

    @pl.when((chunk == 0) & jnp.logical_not(is_prompt))
    def _():
        state_ref[...] = s0_ref[...]

    row = lax.broadcasted_iota(jnp.int32, (CHUNK, CHUNK), 0)
    col = lax.broadcasted_iota(jnp.int32, (CHUNK, CHUNK), 1)
    causal = row >= col
    tril = causal.astype(F32)

    ga = jnp.dot(slab_ref[...], wg_ref[...], precision=HIGHEST, preferred_element_type=F32) + gb_ref[...]
    log_a = (jnp.minimum(ga, 0.0) - jnp.log(1.0 + jnp.exp(-jnp.abs(ga)))) * (1.0 / GLA_TAU)
    b = jnp.dot(tril, log_a, precision=HIGHEST, preferred_element_type=F32)
    b_last = b[CHUNK - 1:CHUNK, :]
    ones = jnp.ones((CHUNK, LANES), F32)
    b_last_col = lax.dot_general(log_a, ones, (((0,), (0,)), ((), ())), precision=HIGHEST,
                                 preferred_element_type=F32)[:, :1]

    qd_all = (q_ref[...] * (GLA_HEAD_K ** -0.5) * jnp.exp(b)).astype(BF16)
    k_all = k_ref[...]
    kd_all = (k_all * jnp.exp(-b)).astype(BF16)
    kt_all = (k_all * jnp.exp(b_last - b)).astype(BF16)
    scale_col = jnp.exp(b_last_col)

    H = range(GLA_HEADS)
    ksl = lambda t, h: t[:, h * GLA_HEAD_K:(h + 1) * GLA_HEAD_K]
    v = [v_ref[:, h * GLA_HEAD_V:(h + 1) * GLA_HEAD_V].astype(BF16) for h in H]
    state = [state_ref[h] for h in H]
    att = [jnp.where(causal, _dot_nt(ksl(qd_all, h), ksl(kd_all, h)), 0.0).astype(BF16) for h in H]
    o_state = [_dot(ksl(qd_all, h), state[h].astype(BF16)) for h in H]
    o = [_dot(att[h], v[h]) + o_state[h] for h in H]
    new_state = [state[h] * scale_col[h * GLA_HEAD_K:(h + 1) * GLA_HEAD_K] + _dot_tn(ksl(kt_all, h), v[h]) for h in H]
    for h in H:
        state_ref[h] = new_state[h]
    outs = []
    for h in H:
        ms = jnp.mean(o[h] * o[h], axis=-1, keepdims=True)
        gate = _silu(r_ref[:, h * GLA_HEAD_V:(h + 1) * GLA_HEAD_V])
        outs.append((o[h] * lax.rsqrt(ms + RMS_EPS) * nw_ref[...] * gate).astype(o_ref.dtype))
    o_ref[...] = jnp.concatenate(outs, axis=1)


def _gla(qkv, slab, gate_r, wg_pad, gate_bias, norm_w, s0, *, steps):
    v_block = 2 * GLA_K_W // GLA_V_W
    state_block = (None, GLA_HEADS, GLA_HEAD_K, GLA_HEAD_V)
    in_specs = [
        pl.BlockSpec((CHUNK, GLA_K_W), lambda s: (s, 0)),
        pl.BlockSpec((CHUNK, GLA_K_W), lambda s: (s, 1)),
        pl.BlockSpec((CHUNK, GLA_V_W), lambda s: (s, v_block)),
        pl.BlockSpec((CHUNK, LANES), lambda s: (s, 0)),
        pl.BlockSpec((CHUNK, GLA_V_W), lambda s: (s, 0)),
        pl.BlockSpec((LANES, GLA_K_W), lambda s: (0, 0)),
        pl.BlockSpec((1, GLA_K_W), lambda s: (0, 0)),
        pl.BlockSpec((1, GLA_HEAD_V), lambda s: (0, 0)),
        pl.BlockSpec(state_block, lambda s: (steps.sample_index(s), 0, 0, 0)),
    ]
    out_specs = [
        pl.BlockSpec((CHUNK, GLA_V_W), lambda s: (s, 0)),
        pl.BlockSpec(state_block, lambda s: (steps.locate(s)[0], 0, 0, 0)),
    ]
    out_shape = [jax.ShapeDtypeStruct((steps.n_steps * CHUNK, GLA_V_W), BF16),
                 jax.ShapeDtypeStruct((steps.bp + steps.bs, GLA_HEADS, GLA_HEAD_K, GLA_HEAD_V), F32)]
    return pl.pallas_call(
        functools.partial(_gla_body, steps=steps), grid=(steps.n_steps,), in_specs=in_specs, out_specs=out_specs,
        out_shape=out_shape, compiler_params=_params("arbitrary"), name="gla_chunk",
    )(qkv, qkv, qkv, slab, gate_r, wg_pad, gate_bias, norm_w, s0)


def _gdn_body(x_ref, slab_ref, z_ref, cw_ref, alog_ref, dtb_ref, nw_ref, conv0_ref, s0_ref,
              o_ref, conv_ref, state_ref, y_ref, *, steps):
    _, chunk, is_prompt = steps.locate(pl.program_id(0))

    @pl.when((chunk == 0) & is_prompt)
    def _():
        conv_ref[...] = jnp.zeros_like(conv_ref)
        state_ref[...] = jnp.zeros_like(state_ref)

    @pl.when((chunk == 0) & jnp.logical_not(is_prompt))
    def _():
        conv_ref[...] = conv0_ref[...]
        state_ref[...] = s0_ref[...]

    x = x_ref[...]
    full = jnp.concatenate([conv_ref[...], x], axis=0)
    conv = cw_ref[GDN_CONV - 1][None] * x
    for d in range(1, GDN_CONV):
        conv = conv + cw_ref[GDN_CONV - 1 - d][None] * full[CONV_PAD_ROWS - d:CONV_PAD_ROWS - d + CHUNK]
    conv_ref[...] = x[CHUNK - CONV_PAD_ROWS:]
    y = _silu(conv)
    n_qk = 2 * GDN_HEADS
    yqk = y[:, :n_qk, :]
    sq = (yqk * yqk).reshape(CHUNK * n_qk, GDN_HEAD_K).astype(BF16)
    ssq = _dot(sq, jnp.ones((GDN_HEAD_K, GDN_HEAD_K), BF16)).reshape(CHUNK, n_qk, GDN_HEAD_K)
    n_sl = y.shape[1]
    yn = jnp.concatenate([yqk * lax.rsqrt(ssq + 1e-6), y[:, n_qk:, :]], axis=1)
    y_ref[...] = yn.reshape(CHUNK * n_sl, GDN_HEAD_K)

    row = lax.broadcasted_iota(jnp.int32, (CHUNK, CHUNK), 0)
    col = lax.broadcasted_iota(jnp.int32, (CHUNK, CHUNK), 1)
    lower = row >= col
    strict = row > col
    eye = (row == col).astype(F32)
    tril = lower.astype(F32)

    slab = slab_ref[...]
    beta_all = _sigmoid(slab)
    lane = lax.broadcasted_iota(jnp.int32, slab.shape, 1)
    is_decay = (lane >= SLAB_DECAY) & (lane < SLAB_DECAY + GDN_HEADS)
    g_all = jnp.where(is_decay, -jnp.exp(alog_ref[...]) * _softplus(slab + dtb_ref[...]), 0.0)
    gc_all = jnp.dot(tril, g_all, precision=HIGHEST, preferred_element_type=F32)
    gc_rows = gc_all.T

    H = range(GDN_HEADS)
    def head_rows(s):
        return y_ref[pl.ds(s, CHUNK, stride=n_sl), :]

    qn = [head_rows(h) * (GDN_HEAD_K ** -0.5) for h in H]
    kn = [head_rows(GDN_HEADS + h) for h in H]
    vs = [head_rows(2 * GDN_HEADS + h) for h in H]
    gcc = [gc_all[:, SLAB_DECAY + h:SLAB_DECAY + h + 1] for h in H]
    gcr = [gc_rows[SLAB_DECAY + h:SLAB_DECAY + h + 1, :] for h in H]
    beta = [beta_all[:, SLAB_BETA + h:SLAB_BETA + h + 1] for h in H]
    egc = [jnp.exp(g) for g in gcc]
    g_last = [g[CHUNK - 1:CHUNK, :] for g in gcc]
    decay = [jnp.exp(jnp.where(lower, gcc[h] - gcr[h], -jnp.inf)) for h in H]
    kb = [k.astype(BF16) for k in kn]
    qb = [q.astype(BF16) for q in qn]
    state = [state_ref[h] for h in H]
    sb = [s.astype(BF16) for s in state]

    a = [_dot_nt(kb[h], kb[h]) * jnp.where(strict, decay[h], 0.0) * beta[h] for h in H]
    qk = [(_dot_nt(qb[h], kb[h]) * decay[h]).astype(BF16) for h in H]
    o_state = [_dot((qn[h] * egc[h]).astype(BF16), sb[h]) for h in H]
    t = [eye - a[h] for h in H]
    apb = [x.astype(BF16) for x in a]
    for _ in range(5):
        apb = [_dot(x, x).astype(BF16) for x in apb]
        t = [t[h] + _dot(t[h].astype(BF16), apb[h]) for h in H]
    rhs = [jnp.concatenate([vs[h] * beta[h], kn[h] * (beta[h] * egc[h])], axis=1).astype(BF16) for h in H]
    uw = [_dot(t[h].astype(BF16), rhs[h]) for h in H]
    v_new = [(uw[h][:, :GDN_HEAD_V] - _dot(uw[h][:, GDN_HEAD_V:].astype(BF16), sb[h])).astype(BF16) for h in H]
    o = [o_state[h] + _dot(qk[h], v_new[h]) for h in H]
    k_tail = [(kn[h] * jnp.exp(g_last[h] - gcc[h])).astype(BF16) for h in H]
    new_state = [state[h] * jnp.exp(g_last[h]) + _dot_tn(k_tail[h], v_new[h]) for h in H]
    for h in H:
        state_ref[h] = new_state[h]
    outs = []
    for h in H:
        ms = jnp.mean(o[h] * o[h], axis=-1, keepdims=True)
        zh = z_ref[:, h * GDN_HEAD_V:(h + 1) * GDN_HEAD_V]
        outs.append((o[h] * lax.rsqrt(ms + RMS_EPS) * nw_ref[...] * _silu(zh)).astype(o_ref.dtype))
    o_ref[...] = jnp.concatenate(outs, axis=1)


def _gdn(qkv, slab, z, conv_w, alog_row, dtb_row, norm_w, conv0, s0, *, steps):
    n_sl = GDN_CONV_CH // LANES
    n_seq = steps.bp + steps.bs
    conv_block = (None, CONV_PAD_ROWS, n_sl, LANES)
    state_block = (None, GDN_HEADS, GDN_HEAD_K, GDN_HEAD_V)
    seq_of = lambda s: steps.locate(s)[0]
    in_specs = [
        pl.BlockSpec((CHUNK, n_sl, LANES), lambda s: (s, 0, 0)),
        pl.BlockSpec((CHUNK, LANES), lambda s: (s, 0)),
        pl.BlockSpec((CHUNK, GDN_V_W), lambda s: (s, 0)),
        pl.BlockSpec((GDN_CONV, n_sl, LANES), lambda s: (0, 0, 0)),
        pl.BlockSpec((1, LANES), lambda s: (0, 0)),
        pl.BlockSpec((1, LANES), lambda s: (0, 0)),
        pl.BlockSpec((1, GDN_HEAD_V), lambda s: (0, 0)),
        pl.BlockSpec(conv_block, lambda s: (steps.sample_index(s), 0, 0, 0)),
        pl.BlockSpec(state_block, lambda s: (steps.sample_index(s), 0, 0, 0)),
    ]
    out_specs = [
        pl.BlockSpec((CHUNK, GDN_V_W), lambda s: (s, 0)),
        pl.BlockSpec(conv_block, lambda s: (seq_of(s), 0, 0, 0)),
        pl.BlockSpec(state_block, lambda s: (seq_of(s), 0, 0, 0)),
    ]
    out_shape = [jax.ShapeDtypeStruct((steps.n_steps * CHUNK, GDN_V_W), BF16),
                 jax.ShapeDtypeStruct((n_seq, CONV_PAD_ROWS, n_sl, LANES), F32),
                 jax.ShapeDtypeStruct((n_seq, GDN_HEADS, GDN_HEAD_K, GDN_HEAD_V), F32)]
    o, conv, state = pl.pallas_call(
        functools.partial(_gdn_body, steps=steps), grid=(steps.n_steps,), in_specs=in_specs, out_specs=out_specs,
        out_shape=out_shape, scratch_shapes=[pltpu.VMEM((CHUNK * n_sl, LANES), F32)],
        compiler_params=_params("arbitrary"), name="gdn_chunk",
    )(qkv.reshape(-1, n_sl, LANES), slab, z, conv_w.reshape(GDN_CONV, n_sl, LANES), alog_row, dtb_row, norm_w,
      conv0.reshape(steps.bs, CONV_PAD_ROWS, n_sl, LANES), s0)
    return o, conv.reshape(n_seq, CONV_PAD_ROWS, GDN_CONV_CH), state


def _split_w_in(w):
    edges = [0]
    for s in IN_SIZES:
        edges.append(edges[-1] + s)
    col = lambda a, b: w[:, :, edges[a]:edges[b]].astype(BF16)
    slab = jnp.zeros((w.shape[0], D_MODEL, LANES), BF16)
    slab = slab.at[:, :, SLAB_GLR:SLAB_GLR + GLA_RANK].set(col(6, 7))
    slab = slab.at[:, :, SLAB_BETA:SLAB_BETA + GDN_HEADS].set(col(9, 10))
    slab = slab.at[:, :, SLAB_DECAY:SLAB_DECAY + GDN_HEADS].set(col(10, 11))
    return dict(swa_q=col(0, 1), swa_kv=col(1, 3), gla_qkv=col(3, 6), slab=slab, gla_r=col(7, 8),
                gdn_qkv=col(8, 9), gdn_z=col(11, 12), gates=col(12, 13))


def _swa_tables(sinks):
    tok = jnp.arange(CHUNK)
    key = jnp.arange(SWA_SEG)
    dist = jnp.abs((tok + SWA_WINDOW)[:, None] - key[None, :]).astype(F32)
    head = (jnp.arange(SWA_KV_HEADS)[:, None, None] * SWA_GROUP + jnp.arange(SWA_PAIRS)[None, :, None] * 2
            + jnp.arange(2)[None, None, :])
    slopes = 2.0 ** (-8.0 * (head + 1).astype(F32) / SWA_HEADS)
    bias = slopes[:, :, None, :, None] * dist[None, None, :, None, :]
    bias = jnp.where(key < SWA_KEYS, bias, SWA_PAD_BIAS)
    bias = bias.reshape(SWA_KV_HEADS, SWA_PAIRS * CHUNK, 2 * SWA_SEG)
    sink = jnp.repeat(sinks.astype(F32)[head], SWA_HEAD_DIM, axis=-1)
    sink = jnp.broadcast_to(sink[:, :, None, :], (SWA_KV_HEADS, SWA_PAIRS, CHUNK, LANES))
    return bias, sink.reshape(SWA_KV_HEADS, SWA_PAIRS * CHUNK, LANES)


def _key_rows(new, cache, *, batch, seq):
    t = new.reshape(batch, seq, SWA_KV_HEADS, SWA_HEAD_DIM)
    if cache is None:
        return jnp.pad(t, ((0, 0), (SWA_WINDOW, 0), (0, 0), (0, 0)))
    return jnp.concatenate([cache, t], axis=1)


def _pad_rows(t, n_rows):
    return jnp.pad(t, ((0, 0), (n_rows - t.shape[1], 0), (0, 0)))


def _layer(x, xb, pe_b, W, WB, layer, last, cache_k, cache_v, gla_s0, gdn_s0, conv_s0, *, bp, sp, bs, ss):
    n_prompt = bp * sp
    steps = _Steps(bp=bp, npc=sp // CHUNK, bs=bs, nsc=ss // CHUNK)
    win = WB['w_in']
    ident = _ep_identity
    swa_q = _matmul(xb, win['swa_q'], layer, [F32], ident, name="proj_swa_q")[0]
    swa_kv = _matmul(xb, win['swa_kv'], layer, [F32], ident, tn=2 * SWA_KV_W, name="proj_swa_kv")[0]
    gla_qkv = _matmul(xb, win['gla_qkv'], layer, [F32], ident, name="proj_gla_qkv")[0]
    slab = _matmul(xb, win['slab'], layer, [F32], ident, tn=LANES, name="proj_slab")[0]
    gla_r = _matmul(xb, win['gla_r'], layer, [F32], ident, name="proj_gla_r")[0]
    gdn_qkv = _matmul(xb, win['gdn_qkv'], layer, [F32], ident, name="proj_gdn_qkv")[0]
    gdn_z = _matmul(xb, win['gdn_z'], layer, [F32], ident, name="proj_gdn_z")[0]

    bias, sink = _swa_tables(W['swa_sinks'])
    new_kv, kv_all = [], []
    for col, cache in ((0, cache_k), (1, cache_v)):
        t = swa_kv[:, col * SWA_KV_W:(col + 1) * SWA_KV_W]
        rows_p = _key_rows(t[:n_prompt], None, batch=bp, seq=sp)
        rows_s = _key_rows(t[n_prompt:], cache, batch=bs, seq=ss)
        new_kv.append((rows_p[:, sp:], rows_s[:, ss:]))
        flat = jnp.concatenate([rows_p.reshape(-1, SWA_KV_HEADS, SWA_HEAD_DIM),
                                rows_s.reshape(-1, SWA_KV_HEADS, SWA_HEAD_DIM)], axis=0)
        flat = jnp.transpose(flat, (1, 0, 2)).astype(BF16)
        kv_all.append(jnp.concatenate([flat, flat], axis=-1))
    o_a = _swa(swa_q, kv_all[0], kv_all[1], bias, sink, steps=steps)

    wg_pad = jnp.zeros((LANES, GLA_K_W), F32).at[SLAB_GLR:SLAB_GLR + GLA_RANK].set(W['gla_w_gate2'])
    gbias = W['gla_gate_bias'].reshape(1, GLA_K_W)
    gnorm = W['gla_norm_w'].reshape(1, GLA_HEAD_V)
    o_b, gla_state = _gla(gla_qkv, slab, gla_r, wg_pad, gbias, gnorm, gla_s0, steps=steps)

    alog_row = jnp.zeros((1, LANES), F32).at[0, SLAB_DECAY:SLAB_DECAY + GDN_HEADS].set(W['gdn_a_log'])
    dtb_row = jnp.zeros((1, LANES), F32).at[0, SLAB_DECAY:SLAB_DECAY + GDN_HEADS].set(W['gdn_dt_bias'])
    dnorm = W['gdn_norm_w'].reshape(1, GDN_HEAD_V)
    o_c, conv_state, gdn_state = _gdn(gdn_qkv, slab, gdn_z, W['gdn_conv_w'], alog_row, dtb_row, dnorm,
                                      _pad_rows(conv_s0, CONV_PAD_ROWS), gdn_s0, steps=steps)
    conv_state = conv_state[:, CONV_PAD_ROWS - (GDN_CONV - 1):]

    merged = _merge(xb, win['gates'], jnp.stack([o_a, o_b, o_c]), WB['w_br'], layer)
    tm = _row_tile(x.shape[0], 512)
    tn = 512
    x, xb = _matmul_ln(merged, WB['w_out'], layer, W['ln1_g'], W['ln1_b'], _ep_residual,
                       [(x, *_tile_spec(tm, tn))], tm=tm, tn=tn, name="out_proj_ln")
    hid = _matmul(xb, WB['w_up'], layer, [BF16], _ep_relu_sq, name="ffn_up")[0]
    tm_down = _row_tile(x.shape[0], 1024)
    z = _matmul(hid, WB['w_down'], layer, [F32], _ep_residual, extras=[(x, *_tile_spec(tm_down, 1024))],
                tm=tm_down, tk=2048, name="ffn_down")[0]
    x, xb = _layer_norm(z, W['ln2_g'], W['ln2_b'])
    pe_extras = [(x, *_tile_spec(tm, tn)),
                 (pe_b, (tm, PE_DIM), lambda i, j, k: (i, 0)),
                 (WB['pe_w_proj'], (None, PE_DIM, tn), lambda i, j, k: (layer, 0, j))]
    pe_gate = functools.partial(_matmul_ln, xb, WB['pe_w_gate'], layer, W['ln3_g'], W['ln3_b'], _ep_pe, pe_extras,
                                tm=tm, tn=tn, name="pe_gate_ln")
    if last:
        x, xb = pe_gate(rows=(0, n_prompt))[0], pe_gate(rows=(n_prompt, x.shape[0] - n_prompt))[0]
    else:
        x, xb = pe_gate()
    states_p = (new_kv[0][0], new_kv[1][0], gla_state[:bp], gdn_state[:bp], conv_state[:bp])
    states_s = (new_kv[0][1], new_kv[1][1], gla_state[bp:], gdn_state[bp:], conv_state[bp:])
    return x, xb, states_p, states_s


def kernel(x_prompt, x_sample, cache_swa_k, cache_swa_v, state_gla, state_gdn, state_gdn_conv, p_prompt, p_sample, w_in, swa_sinks, gla_w_gate2, gla_gate_bias, gla_norm_w, gdn_conv_w, gdn_a_log, gdn_dt_bias, gdn_norm_w, w_br_swa, w_br_gla, w_br_gdn, w_out, ln1_g, ln1_b, w_up, w_down, ln2_g, ln2_b, pe_w_gate, pe_w_proj, ln3_g, ln3_b):
    bp, sp, d = x_prompt.shape
    bs, ss, _ = x_sample.shape
    n_prompt = bp * sp
    x = jnp.concatenate([x_prompt.reshape(n_prompt, d), x_sample.reshape(bs * ss, d)], axis=0)
    xb = x.astype(BF16)
    WB = {'w_in': _split_w_in(w_in), 'w_br': jnp.stack([w_br_swa, w_br_gla, w_br_gdn], axis=1).astype(BF16),
          'w_out': w_out.astype(BF16), 'w_up': w_up.astype(BF16), 'w_down': w_down.astype(BF16),
          'pe_w_gate': pe_w_gate.astype(BF16), 'pe_w_proj': pe_w_proj.astype(BF16)}
    st_p, st_s = [], []
    for l in range(DEPTH):
        W = {'swa_sinks': swa_sinks[l], 'gla_w_gate2': gla_w_gate2[l],
             'gla_gate_bias': gla_gate_bias[l], 'gla_norm_w': gla_norm_w[l], 'gdn_conv_w': gdn_conv_w[l],
             'gdn_a_log': gdn_a_log[l], 'gdn_dt_bias': gdn_dt_bias[l], 'gdn_norm_w': gdn_norm_w[l],
             'ln1_g': ln1_g[l], 'ln1_b': ln1_b[l], 'ln2_g': ln2_g[l], 'ln2_b': ln2_b[l],
             'ln3_g': ln3_g[l], 'ln3_b': ln3_b[l]}
        pe_b = jnp.concatenate([p_prompt[l].reshape(n_prompt, PE_DIM), p_sample[l].reshape(bs * ss, PE_DIM)],
                               axis=0).astype(BF16)
        x, xb, sp_l, ss_l = _layer(x, xb, pe_b, W, WB, l, l == DEPTH - 1, cache_swa_k[l], cache_swa_v[l],
                                   state_gla[l], state_gdn[l], state_gdn_conv[l], bp=bp, sp=sp, bs=bs, ss=ss)
        st_p.append(sp_l)
        st_s.append(ss_l)
    y_prompt = x.reshape(bp, sp, d)
    y_sample = xb.reshape(bs, ss, d)
    stack = lambda sts, i: jnp.stack([s[i] for s in sts])
    return (y_prompt, y_sample,
            stack(st_p, 0), stack(st_p, 1), stack(st_p, 2), stack(st_p, 3), stack(st_p, 4),
            stack(st_s, 0), stack(st_s, 1), stack(st_s, 2), stack(st_s, 3), stack(st_s, 4))
```

```python
import functools
from typing import NamedTuple

import jax
import jax.numpy as jnp
from jax import lax
from jax.experimental import pallas as pl
from jax.experimental.pallas import tpu as pltpu

F32 = jnp.float32
BF16 = jnp.bfloat16
HIGHEST = lax.Precision.HIGHEST

D_MODEL = 4096
DEPTH = 2
CHUNK = 64
PE_DIM = 256
D_FF = 4 * D_MODEL
LN_EPS = 1e-5
RMS_EPS = 1e-6
NEG_INF = -1e30

SWA_HEADS = 32
SWA_KV_HEADS = 4
SWA_GROUP = SWA_HEADS // SWA_KV_HEADS
SWA_HEAD_DIM = 64
SWA_WINDOW = 128
SWA_KEYS = SWA_WINDOW + CHUNK
GLA_HEADS = 4
GLA_HEAD_K = 256
GLA_HEAD_V = 512
GLA_RANK = 16
GLA_TAU = 16.0
GDN_HEADS = 16
GDN_HEAD_K = 128
GDN_HEAD_V = 128
GDN_CONV = 4

SWA_Q_W = SWA_HEADS * SWA_HEAD_DIM
SWA_KV_W = SWA_KV_HEADS * SWA_HEAD_DIM
GLA_K_W = GLA_HEADS * GLA_HEAD_K
GLA_V_W = GLA_HEADS * GLA_HEAD_V
GDN_K_W = GDN_HEADS * GDN_HEAD_K
GDN_V_W = GDN_HEADS * GDN_HEAD_V
GDN_CONV_CH = 2 * GDN_K_W + GDN_V_W
N_BRANCH = 3
IN_SIZES = (SWA_Q_W, SWA_KV_W, SWA_KV_W, GLA_K_W, GLA_K_W, GLA_V_W, GLA_RANK, GLA_V_W,
            GDN_CONV_CH, GDN_HEADS, GDN_HEADS, GDN_V_W, N_BRANCH * D_MODEL)

DN_ALPHA = (2 * DEPTH) ** 0.25

LANES = 128
SUBLANES = 8
VMEM_LIMIT_BYTES = 56 * 1024 * 1024
SLAB_GLR = 0
SLAB_BETA = 16
SLAB_DECAY = 32
CONV_PAD_ROWS = SUBLANES
SWA_PAIRS = SWA_GROUP * SWA_HEAD_DIM // LANES
SWA_SEG = 256
SWA_PAD_BIAS = 1e30


def _params(*sem):
    return pltpu.CompilerParams(dimension_semantics=sem, vmem_limit_bytes=VMEM_LIMIT_BYTES)


def _row_tile(m, want):
    t = want
    while m % t:
        t //= 2
    assert t >= SUBLANES
    return t


class _Steps(NamedTuple):
    bp: int
    npc: int
    bs: int
    nsc: int

    @property
    def n_prompt(self):
        return self.bp * self.npc

    @property
    def n_steps(self):
        return self.bp * self.npc + self.bs * self.nsc

    def locate(self, s):
        is_prompt = s < self.n_prompt
        r = jnp.maximum(s - self.n_prompt, 0)
        seq = jnp.where(is_prompt, s // self.npc, self.bp + r // self.nsc)
        chunk = jnp.where(is_prompt, s % self.npc, r % self.nsc)
        return seq, chunk, is_prompt

    def sample_index(self, s):
        seq, _, _ = self.locate(s)
        return jnp.maximum(seq - self.bp, 0)


def _dot(a, b):
    return jnp.dot(a, b, preferred_element_type=F32)


def _dot_nt(a, b):
    return lax.dot_general(a, b, (((1,), (1,)), ((), ())), preferred_element_type=F32)


def _dot_tn(a, b):
    return lax.dot_general(a, b, (((0,), (0,)), ((), ())), preferred_element_type=F32)


def _sigmoid(x):
    return 0.5 * jnp.tanh(0.5 * x) + 0.5


def _silu(x):
    return x * _sigmoid(x)


def _softplus(x):
    return jnp.maximum(x, 0.0) + jnp.log(1.0 + jnp.exp(-jnp.abs(x)))


def _mm_body(*refs, nk, n_extra, n_out, epilogue):
    a_ref, w_ref = refs[0], refs[1]
    extra = refs[2:2 + n_extra]
    outs = refs[2 + n_extra:2 + n_extra + n_out]

    def finish(acc):
        vals = epilogue(acc, *extra)
        for o, v in zip(outs, vals):
            o[...] = v.astype(o.dtype)

    if nk == 1:
        finish(_dot(a_ref[...], w_ref[...]))
    else:
        acc_ref = refs[-1]
        k = pl.program_id(2)

        @pl.when(k == 0)
        def _():
            acc_ref[...] = jnp.zeros_like(acc_ref)

        acc_ref[...] += _dot(a_ref[...], w_ref[...])

        @pl.when(k == nk - 1)
        def _():
            finish(acc_ref[...])


def _matmul(a, w, layer, out_dtypes, epilogue, extras=(), *, cols=None, tm=1024, tn=1024, tk=None, name):
    M, K = a.shape
    col0, N = (0, w.shape[2]) if cols is None else cols
    tm = _row_tile(M, tm)
    tn = min(tn, N)
    tk = K if tk is None else tk
    assert M % tm == 0 and N % tn == 0 and K % tk == 0 and col0 % tn == 0
    nk = K // tk
    jb0 = col0 // tn
    in_specs = [pl.BlockSpec((tm, tk), lambda i, j, k: (i, k)),
                pl.BlockSpec((None, tk, tn), lambda i, j, k: (layer, k, jb0 + j))]
    args = [a, w]
    for arr, block, imap in extras:
        in_specs.append(pl.BlockSpec(block, imap))
        args.append(arr)
    out_specs = [pl.BlockSpec((tm, tn), lambda i, j, k: (i, j)) for _ in out_dtypes]
    out_shape = [jax.ShapeDtypeStruct((M, N), dt) for dt in out_dtypes]
    scratch = [pltpu.VMEM((tm, tn), F32)] if nk > 1 else []
    body = functools.partial(_mm_body, nk=nk, n_extra=len(extras), n_out=len(out_dtypes), epilogue=epilogue)
    return pl.pallas_call(
        body, grid=(M // tm, N // tn, nk), in_specs=in_specs, out_specs=out_specs, out_shape=out_shape,
        scratch_shapes=scratch, compiler_params=_params("parallel", "parallel", "arbitrary"), name=name,
    )(*args)


def _tile_spec(tm, tn):
    return (tm, tn), (lambda i, j, k: (i, j))


def _ep_identity(acc):
    return (acc,)


def _ep_relu_sq(acc):
    r = jnp.maximum(acc, 0.0)
    return (r * r,)


def _ep_residual(acc, x_ref):
    return (DN_ALPHA * x_ref[...] + acc,)


def _ep_pe(acc, x_ref, pe_ref, wp_ref):
    proj = _dot(pe_ref[...], wp_ref[...])
    return (DN_ALPHA * x_ref[...] + _sigmoid(acc) * proj,)


def _merge_body(x_ref, wg_ref, o_ref, w_ref, out_ref, acc_ref):
    b = pl.program_id(2)
    gate = _sigmoid(_dot(x_ref[...], wg_ref[...]))
    contrib = gate * _dot(o_ref[...], w_ref[...])

    @pl.when(b == 0)
    def _():
        acc_ref[...] = contrib

    @pl.when(b > 0)
    def _():
        acc_ref[...] += contrib

    @pl.when(b == N_BRANCH - 1)
    def _():
        out_ref[...] = acc_ref[...].astype(out_ref.dtype)


def _merge(xb, w_gates, gate_col0, o_stack, w_stack, layer, *, tm=1024, tn=512):
    _, M, K = o_stack.shape
    D = xb.shape[1]
    N = w_stack.shape[3]
    tm = _row_tile(M, tm)
    nj = N // tn
    assert gate_col0 % tn == 0
    jb0 = gate_col0 // tn
    return pl.pallas_call(
        _merge_body, grid=(M // tm, nj, N_BRANCH),
        in_specs=[pl.BlockSpec((tm, D), lambda i, j, b: (i, 0)),
                  pl.BlockSpec((None, D, tn), lambda i, j, b: (layer, 0, jb0 + b * nj + j)),
                  pl.BlockSpec((None, tm, K), lambda i, j, b: (b, i, 0)),
                  pl.BlockSpec((None, None, K, tn), lambda i, j, b: (layer, b, 0, j))],
        out_specs=pl.BlockSpec((tm, tn), lambda i, j, b: (i, j)),
        out_shape=jax.ShapeDtypeStruct((M, N), BF16),
        scratch_shapes=[pltpu.VMEM((tm, tn), F32)],
        compiler_params=_params("parallel", "parallel", "arbitrary"), name="merge_branches",
    )(xb, w_gates, o_stack, w_stack)


def _ln_rows(z, g, b):
    mu = jnp.mean(z, axis=-1, keepdims=True)
    d = z - mu
    var = jnp.mean(d * d, axis=-1, keepdims=True)
    return d * lax.rsqrt(var + LN_EPS) * g + b


def _ln_body(z_ref, g_ref, b_ref, x_ref, xb_ref):
    y = _ln_rows(z_ref[...], g_ref[...], b_ref[...])
    x_ref[...] = y
    xb_ref[...] = y.astype(BF16)


def _layer_norm(z, g, b, *, tm=256):
    M, D = z.shape
    tm = _row_tile(M, tm)
    row = pl.BlockSpec((tm, D), lambda i: (i, 0))
    vec = pl.BlockSpec((1, D), lambda i: (0, 0))
    return pl.pallas_call(
        _ln_body, grid=(M // tm,), in_specs=[row, vec, vec], out_specs=[row, row],
        out_shape=[jax.ShapeDtypeStruct((M, D), F32), jax.ShapeDtypeStruct((M, D), BF16)],
        compiler_params=_params("parallel"), name="layer_norm",
    )(z, g.reshape(1, D), b.reshape(1, D))


def _mm_ln_body(*refs, nj, nk, tn, n_extra, epilogue):
    a_ref, w_ref = refs[0], refs[1]
    extra = refs[2:2 + n_extra]
    g_ref, b_ref, x_ref, xb_ref = refs[2 + n_extra:6 + n_extra]
    j = pl.program_id(1)

    def finish(acc):
        z = epilogue(acc, *extra)[0]
        for jj in range(nj):
            @pl.when(j == jj)
            def _(jj=jj):
                x_ref[:, jj * tn:(jj + 1) * tn] = z

        @pl.when(j == nj - 1)
        def _():
            y = _ln_rows(x_ref[...], g_ref[...], b_ref[...])
            x_ref[...] = y
            xb_ref[...] = y.astype(BF16)

    if nk == 1:
        finish(_dot(a_ref[...], w_ref[...]))
    else:
        acc_ref = refs[-1]
        k = pl.program_id(2)

        @pl.when(k == 0)
        def _():
            acc_ref[...] = jnp.zeros_like(acc_ref)

        acc_ref[...] += _dot(a_ref[...], w_ref[...])

        @pl.when(k == nk - 1)
        def _():
            finish(acc_ref[...])


def _matmul_ln(a, w, layer, g, b, epilogue, extras, *, tm=512, tn=512, tk=None, rows=None, name):
    M, K = a.shape
    N = w.shape[2]
    row0, n_rows = (0, M) if rows is None else rows
    tm = _row_tile(n_rows, tm)
    tk = K if tk is None else tk
    assert N % tn == 0 and K % tk == 0 and row0 % tm == 0
    nj, nk = N // tn, K // tk
    rb0 = row0 // tm
    in_specs = [pl.BlockSpec((tm, tk), lambda i, j, k: (rb0 + i, k)),
                pl.BlockSpec((None, tk, tn), lambda i, j, k: (layer, k, j))]
    args = [a, w]
    for arr, block, imap in extras:
        in_specs.append(pl.BlockSpec(block, lambda i, j, k, imap=imap: imap(rb0 + i, j, k)))
        args.append(arr)
    vec = pl.BlockSpec((1, N), lambda i, j, k: (0, 0))
    row = pl.BlockSpec((tm, N), lambda i, j, k: (i, 0))
    scratch = [pltpu.VMEM((tm, tn), F32)] if nk > 1 else []
    body = functools.partial(_mm_ln_body, nj=nj, nk=nk, tn=tn, n_extra=len(extras), epilogue=epilogue)
    return pl.pallas_call(
        body, grid=(n_rows // tm, nj, nk), in_specs=in_specs + [vec, vec], out_specs=[row, row],
        out_shape=[jax.ShapeDtypeStruct((n_rows, N), F32), jax.ShapeDtypeStruct((n_rows, N), BF16)],
        scratch_shapes=scratch, compiler_params=_params("parallel", "arbitrary", "arbitrary"), name=name,
    )(*args, g.reshape(1, N), b.reshape(1, N))


def _swa_body(*refs, steps, n_sub):
    q_ref, bias_ref, sink_ref, o_ref = refs[0], refs[-3], refs[-2], refs[-1]
    n_win = SWA_KEYS // CHUNK
    k_refs = [refs[1 + 2 * n_win * c:1 + 2 * n_win * c + n_win] for c in range(n_sub)]
    v_refs = [refs[1 + 2 * n_win * c + n_win:1 + 2 * n_win * (c + 1)] for c in range(n_sub)]
    where = [steps.locate(pl.program_id(0) * n_sub + c) for c in range(n_sub)]
    J = [(c, j) for c in range(n_sub) for j in range(SWA_KV_HEADS)]
    C = range(len(J))
    P = range(SWA_PAIRS)
    low = lax.broadcasted_iota(jnp.int32, (SWA_KEYS, LANES), 1) < SWA_HEAD_DIM
    zero = jnp.zeros((SWA_KEYS, LANES), BF16)
    seg_pad = jnp.zeros((SWA_SEG - SWA_KEYS, LANES), BF16)

    def pair_operand(t):
        return jnp.concatenate([jnp.where(low, t, zero), seg_pad, jnp.where(low, zero, t), seg_pad], axis=0)

    def q_tile(c, j, p):
        lane0 = (j * SWA_PAIRS + p) * LANES
        return q_ref[c * CHUNK:(c + 1) * CHUNK, lane0:lane0 + LANES]

    q = [(jnp.concatenate([q_tile(c, j, p) for p in P], axis=0) * (SWA_HEAD_DIM ** -0.5)).astype(BF16) for c, j in J]
    k = [pair_operand(jnp.concatenate([r[j] for r in k_refs[c]], axis=0)) for c, j in J]
    v = [pair_operand(jnp.concatenate([r[j] for r in v_refs[c]], axis=0)) for c, j in J]
    s = [_dot_nt(q[i], k[i]) - bias_ref[J[i][1]] for i in C]
    key = lax.broadcasted_iota(jnp.int32, s[0].shape, 1) & (SWA_SEG - 1)
    valid = [key + jnp.where(is_prompt, chunk * CHUNK, SWA_WINDOW) >= SWA_WINDOW for _, chunk, is_prompt in where]
    s = [jnp.where(valid[J[i][0]], s[i], NEG_INF) for i in C]
    halves = [(x[:, :SWA_SEG], x[:, SWA_SEG:]) for x in s]
    sinks = [(sink_ref[j][:, 0:1], sink_ref[j][:, SWA_HEAD_DIM:SWA_HEAD_DIM + 1]) for _, j in J]
    m = [[jnp.maximum(jnp.max(halves[i][e], axis=-1, keepdims=True), sinks[i][e]) for e in (0, 1)] for i in C]
    p = [[jnp.exp(halves[i][e] - m[i][e]) for e in (0, 1)] for i in C]
    denom = [[jnp.sum(p[i][e], axis=-1, keepdims=True) + jnp.exp(sinks[i][e] - m[i][e]) for e in (0, 1)] for i in C]
    o = [_dot(jnp.concatenate(p[i], axis=1).astype(BF16), v[i]) for i in C]
    even = lax.broadcasted_iota(jnp.int32, o[0].shape, 1) < SWA_HEAD_DIM
    o = [o[i] / jnp.where(even, denom[i][0], denom[i][1]) for i in C]
    for c in range(n_sub):
        tiles = [o[c * SWA_KV_HEADS + j][p * CHUNK:(p + 1) * CHUNK] for j in range(SWA_KV_HEADS) for p in P]
        o_ref[c * CHUNK:(c + 1) * CHUNK, :] = jnp.concatenate(tiles, axis=1).astype(o_ref.dtype)


def _swa(q, k_all, v_all, bias, sink, *, steps):
    back = SWA_WINDOW // CHUNK
    n_sub = 2 if steps.n_steps % 2 == 0 else 1

    def first_key_chunk(s):
        seq, chunk, is_prompt = steps.locate(s)
        prompt_chunks = steps.bp * (steps.npc + back)
        return jnp.where(is_prompt, seq * (steps.npc + back), prompt_chunks + (seq - steps.bp) * (steps.nsc + back)) + chunk

    def win(c, w):
        return pl.BlockSpec((SWA_KV_HEADS, CHUNK, LANES), lambda s: (0, first_key_chunk(s * n_sub + c) + w, 0))

    rows = CHUNK * SWA_PAIRS
    q_spec = pl.BlockSpec((n_sub * CHUNK, SWA_Q_W), lambda s: (s, 0))
    bias_spec = pl.BlockSpec((SWA_KV_HEADS, rows, 2 * SWA_SEG), lambda s: (0, 0, 0))
    sink_spec = pl.BlockSpec((SWA_KV_HEADS, rows, LANES), lambda s: (0, 0, 0))
    n_win = SWA_KEYS // CHUNK
    win_specs, win_args = [], []
    for c in range(n_sub):
        for arr in (k_all, v_all):
            win_specs += [win(c, w) for w in range(n_win)]
            win_args += [arr] * n_win
    return pl.pallas_call(
        functools.partial(_swa_body, steps=steps, n_sub=n_sub), grid=(steps.n_steps // n_sub,),
        in_specs=[q_spec] + win_specs + [bias_spec, sink_spec],
        out_specs=q_spec, out_shape=jax.ShapeDtypeStruct(q.shape, BF16),
        compiler_params=_params("parallel"), name="swa_chunk",
    )(q, *win_args, bias, sink)


def _gla_body(q_ref, k_ref, v_ref, slab_ref, r_ref, wg_ref, gb_ref, nw_ref, s0_ref, o_ref, state_ref, *, steps):
    _, chunk, is_prompt = steps.locate(pl.program_id(0))

    @pl.when((chunk == 0) & is_prompt)
    def _():
        state_ref[...] = jnp.zeros_like(state_ref)

    @pl.when((chunk == 0) & jnp.logical_not(is_prompt))
    def _():
        state_ref[...] = s0_ref[...]

    row = lax.broadcasted_iota(jnp.int32, (CHUNK, CHUNK), 0)
    col = lax.broadcasted_iota(jnp.int32, (CHUNK, CHUNK), 1)
    causal = row >= col
    tril = causal.astype(F32)

    ga = jnp.dot(slab_ref[...], wg_ref[...], precision=HIGHEST, preferred_element_type=F32) + gb_ref[...]
    log_a = (jnp.minimum(ga, 0.0) - jnp.log(1.0 + jnp.exp(-jnp.abs(ga)))) * (1.0 / GLA_TAU)
    b = jnp.dot(tril, log_a, precision=HIGHEST, preferred_element_type=F32)
    b_last = b[CHUNK - 1:CHUNK, :]
    ones = jnp.ones((CHUNK, LANES), F32)
    b_last_col = lax.dot_general(log_a, ones, (((0,), (0,)), ((), ())), precision=HIGHEST,
                                 preferred_element_type=F32)[:, :1]

    qd_all = (q_ref[...] * (GLA_HEAD_K ** -0.5) * jnp.exp(b)).astype(BF16)
    k_all = k_ref[...]
    kd_all = (k_all * jnp.exp(-b)).astype(BF16)
    kt_all = (k_all * jnp.exp(b_last - b)).astype(BF16)
    scale_col = jnp.exp(b_last_col)

    H = range(GLA_HEADS)
    ksl = lambda t, h: t[:, h * GLA_HEAD_K:(h + 1) * GLA_HEAD_K]
    v = [v_ref[:, h * GLA_HEAD_V:(h + 1) * GLA_HEAD_V].astype(BF16) for h in H]
    state = [state_ref[h] for h in H]
    att = [jnp.where(causal, _dot_nt(ksl(qd_all, h), ksl(kd_all, h)), 0.0).astype(BF16) for h in H]
    o_state = [_dot(ksl(qd_all, h), state[h].astype(BF16)) for h in H]
    o = [_dot(att[h], v[h]) + o_state[h] for h in H]
    new_state = [state[h] * scale_col[h * GLA_HEAD_K:(h + 1) * GLA_HEAD_K] + _dot_tn(ksl(kt_all, h), v[h]) for h in H]
    for h in H:
        state_ref[h] = new_state[h]
    outs = []
    for h in H:
        ms = jnp.mean(o[h] * o[h], axis=-1, keepdims=True)
        gate = _silu(r_ref[:, h * GLA_HEAD_V:(h + 1) * GLA_HEAD_V])
        outs.append((o[h] * lax.rsqrt(ms + RMS_EPS) * nw_ref[...] * gate).astype(o_ref.dtype))
    o_ref[...] = jnp.concatenate(outs, axis=1)


def _gla(qkv, slab, gate_r, wg_pad, gate_bias, norm_w, s0, *, steps):
    v_block = 2 * GLA_K_W // GLA_V_W
    state_block = (None, GLA_HEADS, GLA_HEAD_K, GLA_HEAD_V)
    in_specs = [
        pl.BlockSpec((CHUNK, GLA_K_W), lambda s: (s, 0)),
        pl.BlockSpec((CHUNK, GLA_K_W), lambda s: (s, 1)),
        pl.BlockSpec((CHUNK, GLA_V_W), lambda s: (s, v_block)),
        pl.BlockSpec((CHUNK, LANES), lambda s: (s, SLAB_COL // LANES)),
        pl.BlockSpec((CHUNK, GLA_V_W), lambda s: (s, 0)),
        pl.BlockSpec((LANES, GLA_K_W), lambda s: (0, 0)),
        pl.BlockSpec((1, GLA_K_W), lambda s: (0, 0)),
        pl.BlockSpec((1, GLA_HEAD_V), lambda s: (0, 0)),
        pl.BlockSpec(state_block, lambda s: (steps.sample_index(s), 0, 0, 0)),
    ]
    out_specs = [
        pl.BlockSpec((CHUNK, GLA_V_W), lambda s: (s, 0)),
        pl.BlockSpec(state_block, lambda s: (steps.locate(s)[0], 0, 0, 0)),
    ]
    out_shape = [jax.ShapeDtypeStruct((steps.n_steps * CHUNK, GLA_V_W), BF16),
                 jax.ShapeDtypeStruct((steps.bp + steps.bs, GLA_HEADS, GLA_HEAD_K, GLA_HEAD_V), F32)]
    return pl.pallas_call(
        functools.partial(_gla_body, steps=steps), grid=(steps.n_steps,), in_specs=in_specs, out_specs=out_specs,
        out_shape=out_shape, compiler_params=_params("arbitrary"), name="gla_chunk",
    )(qkv, qkv, qkv, slab, gate_r, wg_pad, gate_bias, norm_w, s0)


def _gdn_body(x_ref, slab_ref, z_ref, cw_ref, alog_ref, dtb_ref, nw_ref, conv0_ref, s0_ref,
              o_ref, conv_ref, state_ref, y_ref, *, steps):
    _, chunk, is_prompt = steps.locate(pl.program_id(0))

    @pl.when((chunk == 0) & is_prompt)
    def _():
        conv_ref[...] = jnp.zeros_like(conv_ref)
        state_ref[...] = jnp.zeros_like(state_ref)

    @pl.when((chunk == 0) & jnp.logical_not(is_prompt))
    def _():
        conv_ref[...] = conv0_ref[...]
        state_ref[...] = s0_ref[...]

    x = x_ref[...]
    full = jnp.concatenate([conv_ref[...], x], axis=0)
    conv = cw_ref[GDN_CONV - 1][None] * x
    for d in range(1, GDN_CONV):
        conv = conv + cw_ref[GDN_CONV - 1 - d][None] * full[CONV_PAD_ROWS - d:CONV_PAD_ROWS - d + CHUNK]
    conv_ref[...] = x[CHUNK - CONV_PAD_ROWS:]
    y = _silu(conv)
    n_qk = 2 * GDN_HEADS
    yqk = y[:, :n_qk, :]
    sq = (yqk * yqk).reshape(CHUNK * n_qk, GDN_HEAD_K).astype(BF16)
    ssq = _dot(sq, jnp.ones((GDN_HEAD_K, GDN_HEAD_K), BF16)).reshape(CHUNK, n_qk, GDN_HEAD_K)
    n_sl = y.shape[1]
    yn = jnp.concatenate([yqk * lax.rsqrt(ssq + 1e-6), y[:, n_qk:, :]], axis=1)
    y_ref[...] = yn.reshape(CHUNK * n_sl, GDN_HEAD_K)

    row = lax.broadcasted_iota(jnp.int32, (CHUNK, CHUNK), 0)
    col = lax.broadcasted_iota(jnp.int32, (CHUNK, CHUNK), 1)
    lower = row >= col
    strict = row > col
    eye = (row == col).astype(F32)
    tril = lower.astype(F32)

    slab = slab_ref[...]
    beta_all = _sigmoid(slab)
    lane = lax.broadcasted_iota(jnp.int32, slab.shape, 1)
    is_decay = (lane >= SLAB_DECAY) & (lane < SLAB_DECAY + GDN_HEADS)
    g_all = jnp.where(is_decay, -jnp.exp(alog_ref[...]) * _softplus(slab + dtb_ref[...]), 0.0)
    gc_all = jnp.dot(tril, g_all, precision=HIGHEST, preferred_element_type=F32)
    gc_rows = gc_all.T

    H = range(GDN_HEADS)
    def head_rows(s):
        return y_ref[pl.ds(s, CHUNK, stride=n_sl), :]

    qn = [head_rows(h) * (GDN_HEAD_K ** -0.5) for h in H]
    kn = [head_rows(GDN_HEADS + h) for h in H]
    vs = [head_rows(2 * GDN_HEADS + h) for h in H]
    gcc = [gc_all[:, SLAB_DECAY + h:SLAB_DECAY + h + 1] for h in H]
    gcr = [gc_rows[SLAB_DECAY + h:SLAB_DECAY + h + 1, :] for h in H]
    beta = [beta_all[:, SLAB_BETA + h:SLAB_BETA + h + 1] for h in H]
    egc = [jnp.exp(g) for g in gcc]
    g_last = [g[CHUNK - 1:CHUNK, :] for g in gcc]
    decay = [jnp.exp(jnp.where(lower, gcc[h] - gcr[h], -jnp.inf)) for h in H]
    kb = [k.astype(BF16) for k in kn]
    qb = [q.astype(BF16) for q in qn]
    state = [state_ref[h] for h in H]
    sb = [s.astype(BF16) for s in state]

    a = [_dot_nt(kb[h], kb[h]) * jnp.where(strict, decay[h], 0.0) * beta[h] for h in H]
    qk = [(_dot_nt(qb[h], kb[h]) * decay[h]).astype(BF16) for h in H]
    o_state = [_dot((qn[h] * egc[h]).astype(BF16), sb[h]) for h in H]
    t = [eye - a[h] for h in H]
    apb = [x.astype(BF16) for x in a]
    for _ in range(5):
        apb = [_dot(x, x).astype(BF16) for x in apb]
        t = [t[h] + _dot(t[h].astype(BF16), apb[h]) for h in H]
    rhs = [jnp.concatenate([vs[h] * beta[h], kn[h] * (beta[h] * egc[h])], axis=1).astype(BF16) for h in H]
    uw = [_dot(t[h].astype(BF16), rhs[h]) for h in H]
    v_new = [(uw[h][:, :GDN_HEAD_V] - _dot(uw[h][:, GDN_HEAD_V:].astype(BF16), sb[h])).astype(BF16) for h in H]
    o = [o_state[h] + _dot(qk[h], v_new[h]) for h in H]
    k_tail = [(kn[h] * jnp.exp(g_last[h] - gcc[h])).astype(BF16) for h in H]
    new_state = [state[h] * jnp.exp(g_last[h]) + _dot_tn(k_tail[h], v_new[h]) for h in H]
    for h in H:
        state_ref[h] = new_state[h]
    outs = []
    for h in H:
        ms = jnp.mean(o[h] * o[h], axis=-1, keepdims=True)
        zh = z_ref[:, h * GDN_HEAD_V:(h + 1) * GDN_HEAD_V]
        outs.append((o[h] * lax.rsqrt(ms + RMS_EPS) * nw_ref[...] * _silu(zh)).astype(o_ref.dtype))
    o_ref[...] = jnp.concatenate(outs, axis=1)


def _gdn(qkv, slab, z, conv_w, alog_row, dtb_row, norm_w, conv0, s0, *, steps):
    n_sl = GDN_CONV_CH // LANES
    n_seq = steps.bp + steps.bs
    conv_block = (None, CONV_PAD_ROWS, n_sl, LANES)
    state_block = (None, GDN_HEADS, GDN_HEAD_K, GDN_HEAD_V)
    seq_of = lambda s: steps.locate(s)[0]
    in_specs = [
        pl.BlockSpec((CHUNK, n_sl, LANES), lambda s: (s, 0, 0)),
        pl.BlockSpec((CHUNK, LANES), lambda s: (s, SLAB_COL // LANES)),
        pl.BlockSpec((CHUNK, GDN_V_W), lambda s: (s, 0)),
        pl.BlockSpec((GDN_CONV, n_sl, LANES), lambda s: (0, 0, 0)),
        pl.BlockSpec((1, LANES), lambda s: (0, 0)),
        pl.BlockSpec((1, LANES), lambda s: (0, 0)),
        pl.BlockSpec((1, GDN_HEAD_V), lambda s: (0, 0)),
        pl.BlockSpec(conv_block, lambda s: (steps.sample_index(s), 0, 0, 0)),
        pl.BlockSpec(state_block, lambda s: (steps.sample_index(s), 0, 0, 0)),
    ]
    out_specs = [
        pl.BlockSpec((CHUNK, GDN_V_W), lambda s: (s, 0)),
        pl.BlockSpec(conv_block, lambda s: (seq_of(s), 0, 0, 0)),
        pl.BlockSpec(state_block, lambda s: (seq_of(s), 0, 0, 0)),
    ]
    out_shape = [jax.ShapeDtypeStruct((steps.n_steps * CHUNK, GDN_V_W), BF16),
                 jax.ShapeDtypeStruct((n_seq, CONV_PAD_ROWS, n_sl, LANES), F32),
                 jax.ShapeDtypeStruct((n_seq, GDN_HEADS, GDN_HEAD_K, GDN_HEAD_V), F32)]
    o, conv, state = pl.pallas_call(
        functools.partial(_gdn_body, steps=steps), grid=(steps.n_steps,), in_specs=in_specs, out_specs=out_specs,
        out_shape=out_shape, scratch_shapes=[pltpu.VMEM((CHUNK * n_sl, LANES), F32)],
        compiler_params=_params("arbitrary"), name="gdn_chunk",
    )(qkv.reshape(-1, n_sl, LANES), slab, z, conv_w.reshape(GDN_CONV, n_sl, LANES), alog_row, dtb_row, norm_w,
      conv0.reshape(steps.bs, CONV_PAD_ROWS, n_sl, LANES), s0)
    return o, conv.reshape(n_seq, CONV_PAD_ROWS, GDN_CONV_CH), state


PACK_ALIGN = 1024
KV_SLAB_W = PACK_ALIGN
PACK_KV_SLAB = 0
PACK_SWA_Q = PACK_KV_SLAB + KV_SLAB_W
PACK_GLA_QKV = PACK_SWA_Q + SWA_Q_W
PACK_GLA_R = PACK_GLA_QKV + 2 * GLA_K_W + GLA_V_W
PACK_GDN_QKV = PACK_GLA_R + GLA_V_W
PACK_GDN_Z = PACK_GDN_QKV + GDN_CONV_CH
PACK_GATES = PACK_GDN_Z + GDN_V_W
PACK_COLS = PACK_GATES + N_BRANCH * D_MODEL
SLAB_COL = 2 * SWA_KV_W


def _pack_w_in(w):
    edges = [0]
    for s in IN_SIZES:
        edges.append(edges[-1] + s)
    col = lambda a, b: w[:, :, edges[a]:edges[b]]
    zeros = lambda n: jnp.zeros(w.shape[:2] + (n,), w.dtype)
    slab = [col(6, 7), zeros(SLAB_BETA - SLAB_GLR - GLA_RANK), col(9, 10), zeros(SLAB_DECAY - SLAB_BETA - GDN_HEADS),
            col(10, 11)]
    used = SLAB_COL + SLAB_DECAY + GDN_HEADS
    parts = [col(1, 3)] + slab + [zeros(KV_SLAB_W - used), col(0, 1), col(3, 6), col(7, 8), col(8, 9), col(11, 12),
                                  col(12, 13)]
    packed = jnp.concatenate(parts, axis=-1).astype(BF16)
    assert packed.shape[-1] == PACK_COLS
    return packed


def _swa_tables(sinks):
    tok = jnp.arange(CHUNK)
    key = jnp.arange(SWA_SEG)
    dist = jnp.abs((tok + SWA_WINDOW)[:, None] - key[None, :]).astype(F32)
    head = (jnp.arange(SWA_KV_HEADS)[:, None, None] * SWA_GROUP + jnp.arange(SWA_PAIRS)[None, :, None] * 2
            + jnp.arange(2)[None, None, :])
    slopes = 2.0 ** (-8.0 * (head + 1).astype(F32) / SWA_HEADS)
    bias = slopes[:, :, None, :, None] * dist[None, None, :, None, :]
    bias = jnp.where(key < SWA_KEYS, bias, SWA_PAD_BIAS)
    bias = bias.reshape(SWA_KV_HEADS, SWA_PAIRS * CHUNK, 2 * SWA_SEG)
    sink = jnp.repeat(sinks.astype(F32)[head], SWA_HEAD_DIM, axis=-1)
    sink = jnp.broadcast_to(sink[:, :, None, :], (SWA_KV_HEADS, SWA_PAIRS, CHUNK, LANES))
    return bias, sink.reshape(SWA_KV_HEADS, SWA_PAIRS * CHUNK, LANES)


def _key_rows(new, cache, *, batch, seq):
    t = new.reshape(batch, seq, SWA_KV_HEADS, SWA_HEAD_DIM)
    if cache is None:
        return jnp.pad(t, ((0, 0), (SWA_WINDOW, 0), (0, 0), (0, 0)))
    return jnp.concatenate([cache, t], axis=1)


def _pad_rows(t, n_rows):
    return jnp.pad(t, ((0, 0), (n_rows - t.shape[1], 0), (0, 0)))


def _layer(x, xb, pe_b, W, WB, layer, last, cache_k, cache_v, gla_s0, gdn_s0, conv_s0, *, bp, sp, bs, ss):
    n_prompt = bp * sp
    steps = _Steps(bp=bp, npc=sp // CHUNK, bs=bs, nsc=ss // CHUNK)
    win = WB['w_in']
    ident = _ep_identity
    proj = lambda col0, width, name: _matmul(xb, win, layer, [F32], ident, cols=(col0, width), name=name)[0]
    kv_slab = proj(PACK_KV_SLAB, KV_SLAB_W, "proj_kv_slab")
    swa_q = proj(PACK_SWA_Q, SWA_Q_W, "proj_swa_q")
    gla_qkv = proj(PACK_GLA_QKV, 2 * GLA_K_W + GLA_V_W, "proj_gla_qkv")
    gla_r = proj(PACK_GLA_R, GLA_V_W, "proj_gla_r")
    gdn_qkv = proj(PACK_GDN_QKV, GDN_CONV_CH, "proj_gdn_qkv")
    gdn_z = proj(PACK_GDN_Z, GDN_V_W, "proj_gdn_z")

    bias, sink = _swa_tables(W['swa_sinks'])
    new_kv, kv_all = [], []
    for col, cache in ((0, cache_k), (1, cache_v)):
        t = kv_slab[:, col * SWA_KV_W:(col + 1) * SWA_KV_W]
        rows_p = _key_rows(t[:n_prompt], None, batch=bp, seq=sp)
        rows_s = _key_rows(t[n_prompt:], cache, batch=bs, seq=ss)
        new_kv.append((rows_p[:, sp:], rows_s[:, ss:]))
        flat = jnp.concatenate([rows_p.reshape(-1, SWA_KV_HEADS, SWA_HEAD_DIM),
                                rows_s.reshape(-1, SWA_KV_HEADS, SWA_HEAD_DIM)], axis=0)
        flat = jnp.transpose(flat, (1, 0, 2)).astype(BF16)
        kv_all.append(jnp.concatenate([flat, flat], axis=-1))
    o_a = _swa(swa_q, kv_all[0], kv_all[1], bias, sink, steps=steps)

    wg_pad = jnp.zeros((LANES, GLA_K_W), F32).at[SLAB_GLR:SLAB_GLR + GLA_RANK].set(W['gla_w_gate2'])
    gbias = W['gla_gate_bias'].reshape(1, GLA_K_W)
    gnorm = W['gla_norm_w'].reshape(1, GLA_HEAD_V)
    o_b, gla_state = _gla(gla_qkv, kv_slab, gla_r, wg_pad, gbias, gnorm, gla_s0, steps=steps)

    alog_row = jnp.zeros((1, LANES), F32).at[0, SLAB_DECAY:SLAB_DECAY + GDN_HEADS].set(W['gdn_a_log'])
    dtb_row = jnp.zeros((1, LANES), F32).at[0, SLAB_DECAY:SLAB_DECAY + GDN_HEADS].set(W['gdn_dt_bias'])
    dnorm = W['gdn_norm_w'].reshape(1, GDN_HEAD_V)
    o_c, conv_state, gdn_state = _gdn(gdn_qkv, kv_slab, gdn_z, W['gdn_conv_w'], alog_row, dtb_row, dnorm,
                                      _pad_rows(conv_s0, CONV_PAD_ROWS), gdn_s0, steps=steps)
    conv_state = conv_state[:, CONV_PAD_ROWS - (GDN_CONV - 1):]

    merged = _merge(xb, win, PACK_GATES, jnp.stack([o_a, o_b, o_c]), WB['w_br'], layer)
    tm = _row_tile(x.shape[0], 512)
    tn = 512
    x, xb = _matmul_ln(merged, WB['w_out'], layer, W['ln1_g'], W['ln1_b'], _ep_residual,
                       [(x, *_tile_spec(tm, tn))], tm=tm, tn=tn, name="out_proj_ln")
    hid = _matmul(xb, WB['w_up'], layer, [BF16], _ep_relu_sq, name="ffn_up")[0]
    tm_down = _row_tile(x.shape[0], 1024)
    z = _matmul(hid, WB['w_down'], layer, [F32], _ep_residual, extras=[(x, *_tile_spec(tm_down, 1024))],
                tm=tm_down, tk=2048, name="ffn_down")[0]
    x, xb = _layer_norm(z, W['ln2_g'], W['ln2_b'])
    pe_extras = [(x, *_tile_spec(tm, tn)),
                 (pe_b, (tm, PE_DIM), lambda i, j, k: (i, 0)),
                 (WB['pe_w_proj'], (None, PE_DIM, tn), lambda i, j, k: (layer, 0, j))]
    pe_gate = functools.partial(_matmul_ln, xb, WB['pe_w_gate'], layer, W['ln3_g'], W['ln3_b'], _ep_pe, pe_extras,
                                tm=tm, tn=tn, name="pe_gate_ln")
    if last:
        x, xb = pe_gate(rows=(0, n_prompt))[0], pe_gate(rows=(n_prompt, x.shape[0] - n_prompt))[0]
    else:
        x, xb = pe_gate()
    states_p = (new_kv[0][0], new_kv[1][0], gla_state[:bp], gdn_state[:bp], conv_state[:bp])
    states_s = (new_kv[0][1], new_kv[1][1], gla_state[bp:], gdn_state[bp:], conv_state[bp:])
    return x, xb, states_p, states_s


def kernel(x_prompt, x_sample, cache_swa_k, cache_swa_v, state_gla, state_gdn, state_gdn_conv, p_prompt, p_sample, w_in, swa_sinks, gla_w_gate2, gla_gate_bias, gla_norm_w, gdn_conv_w, gdn_a_log, gdn_dt_bias, gdn_norm_w, w_br_swa, w_br_gla, w_br_gdn, w_out, ln1_g, ln1_b, w_up, w_down, ln2_g, ln2_b, pe_w_gate, pe_w_proj, ln3_g, ln3_b):
    bp, sp, d = x_prompt.shape
    bs, ss, _ = x_sample.shape
    n_prompt = bp * sp
    x = jnp.concatenate([x_prompt.reshape(n_prompt, d), x_sample.reshape(bs * ss, d)], axis=0)
    xb = x.astype(BF16)
    WB = {'w_in': _pack_w_in(w_in),'w_br': jnp.stack([w_br_swa, w_br_gla, w_br_gdn], axis=1).astype(BF16),
          'w_out': w_out.astype(BF16), 'w_up': w_up.astype(BF16), 'w_down': w_down.astype(BF16),
          'pe_w_gate': pe_w_gate.astype(BF16), 'pe_w_proj': pe_w_proj.astype(BF16)}
    st_p, st_s = [], []
    for l in range(DEPTH):
        W = {'swa_sinks': swa_sinks[l], 'gla_w_gate2': gla_w_gate2[l],
             'gla_gate_bias': gla_gate_bias[l], 'gla_norm_w': gla_norm_w[l], 'gdn_conv_w': gdn_conv_w[l],
             'gdn_a_log': gdn_a_log[l], 'gdn_dt_bias': gdn_dt_bias[l], 'gdn_norm_w': gdn_norm_w[l],
             'ln1_g': ln1_g[l], 'ln1_b': ln1_b[l], 'ln2_g': ln2_g[l], 'ln2_b': ln2_b[l],
             'ln3_g': ln3_g[l], 'ln3_b': ln3_b[l]}
        pe_b = jnp.concatenate([p_prompt[l].reshape(n_prompt, PE_DIM), p_sample[l].reshape(bs * ss, PE_DIM)],
                               axis=0).astype(BF16)
        x, xb, sp_l, ss_l = _layer(x, xb, pe_b, W, WB, l, l == DEPTH - 1, cache_swa_k[l], cache_swa_v[l],
                                   state_gla[l], state_gdn[l], state_gdn_conv[l], bp=bp, sp=sp, bs=bs, ss=ss)
        st_p.append(sp_l)
        st_s.append(ss_l)
    y_prompt = x.reshape(bp, sp, d)
    y_sample = xb.reshape(bs, ss, d)
    stack = lambda sts, i: jnp.stack([s[i] for s in sts])
    return (y_prompt, y_sample,
            stack(st_p, 0), stack(st_p, 1), stack(st_p, 2), stack(st_p, 3), stack(st_p, 4),
            stack(st_s, 0), stack(st_s, 1), stack(st_s, 2), stack(st_s, 3), stack(st_s, 4))
```

```python
import functools
from typing import NamedTuple

import jax
import jax.numpy as jnp
from jax import lax
from jax.experimental import pallas as pl
from jax.experimental.pallas import tpu as pltpu

F32 = jnp.float32
BF16 = jnp.bfloat16
HIGHEST = lax.Precision.HIGHEST

D_MODEL = 4096
DEPTH = 2
CHUNK = 64
PE_DIM = 256
D_FF = 4 * D_MODEL
LN_EPS = 1e-5
RMS_EPS = 1e-6
NEG_INF = -1e30

SWA_HEADS = 32
SWA_KV_HEADS = 4
SWA_GROUP = SWA_HEADS // SWA_KV_HEADS
SWA_HEAD_DIM = 64
SWA_WINDOW = 128
SWA_KEYS = SWA_WINDOW + CHUNK
GLA_HEADS = 4
GLA_HEAD_K = 256
GLA_HEAD_V = 512
GLA_RANK = 16
GLA_TAU = 16.0
GDN_HEADS = 16
GDN_HEAD_K = 128
GDN_HEAD_V = 128
GDN_CONV = 4

SWA_Q_W = SWA_HEADS * SWA_HEAD_DIM
SWA_KV_W = SWA_KV_HEADS * SWA_HEAD_DIM
GLA_K_W = GLA_HEADS * GLA_HEAD_K
GLA_V_W = GLA_HEADS * GLA_HEAD_V
GDN_K_W = GDN_HEADS * GDN_HEAD_K
GDN_V_W = GDN_HEADS * GDN_HEAD_V
GDN_CONV_CH = 2 * GDN_K_W + GDN_V_W
N_BRANCH = 3
IN_SIZES = (SWA_Q_W, SWA_KV_W, SWA_KV_W, GLA_K_W, GLA_K_W, GLA_V_W, GLA_RANK, GLA_V_W,
            GDN_CONV_CH, GDN_HEADS, GDN_HEADS, GDN_V_W, N_BRANCH * D_MODEL)

DN_ALPHA = (2 * DEPTH) ** 0.25

LANES = 128
SUBLANES = 8
VMEM_LIMIT_BYTES = 56 * 1024 * 1024
SLAB_GLR = 0
SLAB_BETA = 16
SLAB_DECAY = 32
CONV_PAD_ROWS = SUBLANES
SWA_PAIRS = SWA_GROUP * SWA_HEAD_DIM // LANES
SWA_SEG = 256
SWA_PAD_BIAS = 1e30


def _params(*sem):
    return pltpu.CompilerParams(dimension_semantics=sem, vmem_limit_bytes=VMEM_LIMIT_BYTES)


def _row_tile(m, want):
    t = want
    while m % t:
        t //= 2
    assert t >= SUBLANES
    return t


class _Steps(NamedTuple):
    bp: int
    npc: int
    bs: int
    nsc: int

    @property
    def n_prompt(self):
        return self.bp * self.npc

    @property
    def n_steps(self):
        return self.bp * self.npc + self.bs * self.nsc

    def locate(self, s):
        is_prompt = s < self.n_prompt
        r = jnp.maximum(s - self.n_prompt, 0)
        seq = jnp.where(is_prompt, s // self.npc, self.bp + r // self.nsc)
        chunk = jnp.where(is_prompt, s % self.npc, r % self.nsc)
        return seq, chunk, is_prompt

    def sample_index(self, s):
        seq, _, _ = self.locate(s)
        return jnp.maximum(seq - self.bp, 0)


def _dot(a, b):
    return jnp.dot(a, b, preferred_element_type=F32)


def _dot_nt(a, b):
    return lax.dot_general(a, b, (((1,), (1,)), ((), ())), preferred_element_type=F32)


def _dot_tn(a, b):
    return lax.dot_general(a, b, (((0,), (0,)), ((), ())), preferred_element_type=F32)


def _sigmoid(x):
    return 0.5 * jnp.tanh(0.5 * x) + 0.5


def _silu(x):
    return x * _sigmoid(x)


def _softplus(x):
    return jnp.maximum(x, 0.0) + jnp.log(1.0 + jnp.exp(-jnp.abs(x)))


def _mm_body(*refs, nk, n_extra, n_out, epilogue, w_rows):
    a_ref, w_ref = refs[0], refs[1]
    extra = refs[2:2 + n_extra]
    outs = refs[2 + n_extra:2 + n_extra + n_out]
    dot = _dot_nt if w_rows else _dot

    def finish(acc):
        vals = epilogue(acc, *extra)
        for o, v in zip(outs, vals):
            o[...] = v.astype(o.dtype)

    if nk == 1:
        finish(dot(a_ref[...], w_ref[...]))
    else:
        acc_ref = refs[-1]
        k = pl.program_id(2)

        @pl.when(k == 0)
        def _():
            acc_ref[...] = jnp.zeros_like(acc_ref)

        acc_ref[...] += dot(a_ref[...], w_ref[...])

        @pl.when(k == nk - 1)
        def _():
            finish(acc_ref[...])


def _matmul(a, w, layer, out_dtypes, epilogue, extras=(), *, w_rows=False, tm=1024, tn=1024, tk=None, name):
    M, K = a.shape
    N = w.shape[1] if w_rows else w.shape[2]
    tm = _row_tile(M, tm)
    tn = min(tn, N)
    tk = K if tk is None else tk
    assert M % tm == 0 and N % tn == 0 and K % tk == 0
    nk = K // tk
    w_spec = (pl.BlockSpec((None, tn, tk), lambda i, j, k: (layer, j, k)) if w_rows
              else pl.BlockSpec((None, tk, tn), lambda i, j, k: (layer, k, j)))
    in_specs = [pl.BlockSpec((tm, tk), lambda i, j, k: (i, k)), w_spec]
    args = [a, w]
    for arr, block, imap in extras:
        in_specs.append(pl.BlockSpec(block, imap))
        args.append(arr)
    out_specs = [pl.BlockSpec((tm, tn), lambda i, j, k: (i, j)) for _ in out_dtypes]
    out_shape = [jax.ShapeDtypeStruct((M, N), dt) for dt in out_dtypes]
    scratch = [pltpu.VMEM((tm, tn), F32)] if nk > 1 else []
    body = functools.partial(_mm_body, nk=nk, n_extra=len(extras), n_out=len(out_dtypes), epilogue=epilogue,
                             w_rows=w_rows)
    return pl.pallas_call(
        body, grid=(M // tm, N // tn, nk), in_specs=in_specs, out_specs=out_specs, out_shape=out_shape,
        scratch_shapes=scratch, compiler_params=_params("parallel", "parallel", "arbitrary"), name=name,
    )(*args)


def _tile_spec(tm, tn):
    return (tm, tn), (lambda i, j, k: (i, j))


def _ep_identity(acc):
    return (acc,)


def _ep_relu_sq(acc):
    r = jnp.maximum(acc, 0.0)
    return (r * r,)


def _ep_residual(acc, x_ref):
    return (DN_ALPHA * x_ref[...] + acc,)


def _ep_pe(acc, x_ref, pe_ref, wp_ref):
    proj = _dot(pe_ref[...], wp_ref[...])
    return (DN_ALPHA * x_ref[...] + _sigmoid(acc) * proj,)


def _merge_body(x_ref, wg_ref, o_ref, w_ref, out_ref, acc_ref):
    b = pl.program_id(2)
    gate = _sigmoid(_dot_nt(x_ref[...], wg_ref[...]))
    contrib = gate * _dot(o_ref[...], w_ref[...])

    @pl.when(b == 0)
    def _():
        acc_ref[...] = contrib

    @pl.when(b > 0)
    def _():
        acc_ref[...] += contrib

    @pl.when(b == N_BRANCH - 1)
    def _():
        out_ref[...] = acc_ref[...].astype(out_ref.dtype)


def _merge(xb, w_gates, o_stack, w_stack, layer, *, tm=1024, tn=512):
    _, M, K = o_stack.shape
    D = xb.shape[1]
    N = w_stack.shape[3]
    tm = _row_tile(M, tm)
    nj = N // tn
    return pl.pallas_call(
        _merge_body, grid=(M // tm, nj, N_BRANCH),
        in_specs=[pl.BlockSpec((tm, D), lambda i, j, b: (i, 0)),
                  pl.BlockSpec((None, tn, D), lambda i, j, b: (layer, b * nj + j, 0)),
                  pl.BlockSpec((None, tm, K), lambda i, j, b: (b, i, 0)),
                  pl.BlockSpec((None, None, K, tn), lambda i, j, b: (layer, b, 0, j))],
        out_specs=pl.BlockSpec((tm, tn), lambda i, j, b: (i, j)),
        out_shape=jax.ShapeDtypeStruct((M, N), BF16),
        scratch_shapes=[pltpu.VMEM((tm, tn), F32)],
        compiler_params=_params("parallel", "parallel", "arbitrary"), name="merge_branches",
    )(xb, w_gates, o_stack, w_stack)


def _ln_rows(z, g, b):
    mu = jnp.mean(z, axis=-1, keepdims=True)
    d = z - mu
    var = jnp.mean(d * d, axis=-1, keepdims=True)
    return d * lax.rsqrt(var + LN_EPS) * g + b


def _ln_body(z_ref, g_ref, b_ref, x_ref, xb_ref):
    y = _ln_rows(z_ref[...], g_ref[...], b_ref[...])
    x_ref[...] = y
    xb_ref[...] = y.astype(BF16)


def _layer_norm(z, g, b, *, tm=256):
    M, D = z.shape
    tm = _row_tile(M, tm)
    row = pl.BlockSpec((tm, D), lambda i: (i, 0))
    vec = pl.BlockSpec((1, D), lambda i: (0, 0))
    return pl.pallas_call(
        _ln_body, grid=(M // tm,), in_specs=[row, vec, vec], out_specs=[row, row],
        out_shape=[jax.ShapeDtypeStruct((M, D), F32), jax.ShapeDtypeStruct((M, D), BF16)],
        compiler_params=_params("parallel"), name="layer_norm",
    )(z, g.reshape(1, D), b.reshape(1, D))


def _mm_ln_body(*refs, nj, nk, tn, n_extra, epilogue):
    a_ref, w_ref = refs[0], refs[1]
    extra = refs[2:2 + n_extra]
    g_ref, b_ref, x_ref, xb_ref = refs[2 + n_extra:6 + n_extra]
    j = pl.program_id(1)

    def finish(acc):
        z = epilogue(acc, *extra)[0]
        for jj in range(nj):
            @pl.when(j == jj)
            def _(jj=jj):
                x_ref[:, jj * tn:(jj + 1) * tn] = z

        @pl.when(j == nj - 1)
        def _():
            y = _ln_rows(x_ref[...], g_ref[...], b_ref[...])
            x_ref[...] = y
            xb_ref[...] = y.astype(BF16)

    if nk == 1:
        finish(_dot(a_ref[...], w_ref[...]))
    else:
        acc_ref = refs[-1]
        k = pl.program_id(2)

        @pl.when(k == 0)
        def _():
            acc_ref[...] = jnp.zeros_like(acc_ref)

        acc_ref[...] += _dot(a_ref[...], w_ref[...])

        @pl.when(k == nk - 1)
        def _():
            finish(acc_ref[...])


def _matmul_ln(a, w, layer, g, b, epilogue, extras, *, tm=512, tn=512, tk=None, rows=None, name):
    M, K = a.shape
    N = w.shape[2]
    row0, n_rows = (0, M) if rows is None else rows
    tm = _row_tile(n_rows, tm)
    tk = K if tk is None else tk
    assert N % tn == 0 and K % tk == 0 and row0 % tm == 0
    nj, nk = N // tn, K // tk
    rb0 = row0 // tm
    in_specs = [pl.BlockSpec((tm, tk), lambda i, j, k: (rb0 + i, k)),
                pl.BlockSpec((None, tk, tn), lambda i, j, k: (layer, k, j))]
    args = [a, w]
    for arr, block, imap in extras:
        in_specs.append(pl.BlockSpec(block, lambda i, j, k, imap=imap: imap(rb0 + i, j, k)))
        args.append(arr)
    vec = pl.BlockSpec((1, N), lambda i, j, k: (0, 0))
    row = pl.BlockSpec((tm, N), lambda i, j, k: (i, 0))
    scratch = [pltpu.VMEM((tm, tn), F32)] if nk > 1 else []
    body = functools.partial(_mm_ln_body, nj=nj, nk=nk, tn=tn, n_extra=len(extras), epilogue=epilogue)
    return pl.pallas_call(
        body, grid=(n_rows // tm, nj, nk), in_specs=in_specs + [vec, vec], out_specs=[row, row],
        out_shape=[jax.ShapeDtypeStruct((n_rows, N), F32), jax.ShapeDtypeStruct((n_rows, N), BF16)],
        scratch_shapes=scratch, compiler_params=_params("parallel", "arbitrary", "arbitrary"), name=name,
    )(*args, g.reshape(1, N), b.reshape(1, N))


def _swa_body(*refs, steps, n_sub):
    q_ref, bias_ref, sink_ref, o_ref = refs[0], refs[-3], refs[-2], refs[-1]
    n_win = SWA_KEYS // CHUNK
    k_refs = [refs[1 + 2 * n_win * c:1 + 2 * n_win * c + n_win] for c in range(n_sub)]
    v_refs = [refs[1 + 2 * n_win * c + n_win:1 + 2 * n_win * (c + 1)] for c in range(n_sub)]
    where = [steps.locate(pl.program_id(0) * n_sub + c) for c in range(n_sub)]
    J = [(c, j) for c in range(n_sub) for j in range(SWA_KV_HEADS)]
    C = range(len(J))
    P = range(SWA_PAIRS)
    low = lax.broadcasted_iota(jnp.int32, (SWA_KEYS, LANES), 1) < SWA_HEAD_DIM
    zero = jnp.zeros((SWA_KEYS, LANES), BF16)
    seg_pad = jnp.zeros((SWA_SEG - SWA_KEYS, LANES), BF16)

    def pair_operand(t):
        return jnp.concatenate([jnp.where(low, t, zero), seg_pad, jnp.where(low, zero, t), seg_pad], axis=0)

    def q_tile(c, j, p):
        lane0 = (j * SWA_PAIRS + p) * LANES
        return q_ref[c * CHUNK:(c + 1) * CHUNK, lane0:lane0 + LANES]

    q = [(jnp.concatenate([q_tile(c, j, p) for p in P], axis=0) * (SWA_HEAD_DIM ** -0.5)).astype(BF16) for c, j in J]
    k = [pair_operand(jnp.concatenate([r[j] for r in k_refs[c]], axis=0)) for c, j in J]
    v = [pair_operand(jnp.concatenate([r[j] for r in v_refs[c]], axis=0)) for c, j in J]
    s = [_dot_nt(q[i], k[i]) - bias_ref[J[i][1]] for i in C]
    key = lax.broadcasted_iota(jnp.int32, s[0].shape, 1) & (SWA_SEG - 1)
    valid = [key + jnp.where(is_prompt, chunk * CHUNK, SWA_WINDOW) >= SWA_WINDOW for _, chunk, is_prompt in where]
    s = [jnp.where(valid[J[i][0]], s[i], NEG_INF) for i in C]
    halves = [(x[:, :SWA_SEG], x[:, SWA_SEG:]) for x in s]
    sinks = [(sink_ref[j][:, 0:1], sink_ref[j][:, SWA_HEAD_DIM:SWA_HEAD_DIM + 1]) for _, j in J]
    m = [[jnp.maximum(jnp.max(halves[i][e], axis=-1, keepdims=True), sinks[i][e]) for e in (0, 1)] for i in C]
    p = [[jnp.exp(halves[i][e] - m[i][e]) for e in (0, 1)] for i in C]
    denom = [[jnp.sum(p[i][e], axis=-1, keepdims=True) + jnp.exp(sinks[i][e] - m[i][e]) for e in (0, 1)] for i in C]
    o = [_dot(jnp.concatenate(p[i], axis=1).astype(BF16), v[i]) for i in C]
    even = lax.broadcasted_iota(jnp.int32, o[0].shape, 1) < SWA_HEAD_DIM
    o = [o[i] / jnp.where(even, denom[i][0], denom[i][1]) for i in C]
    for c in range(n_sub):
        tiles = [o[c * SWA_KV_HEADS + j][p * CHUNK:(p + 1) * CHUNK] for j in range(SWA_KV_HEADS) for p in P]
        o_ref[c * CHUNK:(c + 1) * CHUNK, :] = jnp.concatenate(tiles, axis=1).astype(o_ref.dtype)


def _swa(q, k_all, v_all, bias, sink, *, steps):
    back = SWA_WINDOW // CHUNK
    n_sub = 2 if steps.n_steps % 2 == 0 else 1

    def first_key_chunk(s):
        seq, chunk, is_prompt = steps.locate(s)
        prompt_chunks = steps.bp * (steps.npc + back)
        return jnp.where(is_prompt, seq * (steps.npc + back), prompt_chunks + (seq - steps.bp) * (steps.nsc + back)) + chunk

    def win(c, w):
        return pl.BlockSpec((SWA_KV_HEADS, CHUNK, LANES), lambda s: (0, first_key_chunk(s * n_sub + c) + w, 0))

    rows = CHUNK * SWA_PAIRS
    q_spec = pl.BlockSpec((n_sub * CHUNK, SWA_Q_W), lambda s: (s, 0))
    bias_spec = pl.BlockSpec((SWA_KV_HEADS, rows, 2 * SWA_SEG), lambda s: (0, 0, 0))
    sink_spec = pl.BlockSpec((SWA_KV_HEADS, rows, LANES), lambda s: (0, 0, 0))
    n_win = SWA_KEYS // CHUNK
    win_specs, win_args = [], []
    for c in range(n_sub):
        for arr in (k_all, v_all):
            win_specs += [win(c, w) for w in range(n_win)]
            win_args += [arr] * n_win
    return pl.pallas_call(
        functools.partial(_swa_body, steps=steps, n_sub=n_sub), grid=(steps.n_steps // n_sub,),
        in_specs=[q_spec] + win_specs + [bias_spec, sink_spec],
        out_specs=q_spec, out_shape=jax.ShapeDtypeStruct(q.shape, BF16),
        compiler_params=_params("parallel"), name="swa_chunk",
    )(q, *win_args, bias, sink)


def _gla_body(q_ref, k_ref, v_ref, slab_ref, r_ref, wg_ref, gb_ref, nw_ref, s0_ref, o_ref, state_ref, *, steps):
    _, chunk, is_prompt = steps.locate(pl.program_id(0))

    @pl.when((chunk == 0) & is_prompt)
    def _():
        state_ref[...] = jnp.zeros_like(state_ref)

    @pl.when((chunk == 0) & jnp.logical_not(is_prompt))
    def _():
        state_ref[...] = s0_ref[...]

    row = lax.broadcasted_iota(jnp.int32, (CHUNK, CHUNK), 0)
    col = lax.broadcasted_iota(jnp.int32, (CHUNK, CHUNK), 1)
    causal = row >= col
    tril = causal.astype(F32)

    ga = jnp.dot(slab_ref[...], wg_ref[...], precision=HIGHEST, preferred_element_type=F32) + gb_ref[...]
    log_a = (jnp.minimum(ga, 0.0) - jnp.log(1.0 + jnp.exp(-jnp.abs(ga)))) * (1.0 / GLA_TAU)
    b = jnp.dot(tril, log_a, precision=HIGHEST, preferred_element_type=F32)
    b_last = b[CHUNK - 1:CHUNK, :]
    ones = jnp.ones((CHUNK, LANES), F32)
    b_last_col = lax.dot_general(log_a, ones, (((0,), (0,)), ((), ())), precision=HIGHEST,
                                 preferred_element_type=F32)[:, :1]

    qd_all = (q_ref[...] * (GLA_HEAD_K ** -0.5) * jnp.exp(b)).astype(BF16)
    k_all = k_ref[...]
    kd_all = (k_all * jnp.exp(-b)).astype(BF16)
    kt_all = (k_all * jnp.exp(b_last - b)).astype(BF16)
    scale_col = jnp.exp(b_last_col)

    H = range(GLA_HEADS)
    ksl = lambda t, h: t[:, h * GLA_HEAD_K:(h + 1) * GLA_HEAD_K]
    v = [v_ref[:, h * GLA_HEAD_V:(h + 1) * GLA_HEAD_V].astype(BF16) for h in H]
    state = [state_ref[h] for h in H]
    att = [jnp.where(causal, _dot_nt(ksl(qd_all, h), ksl(kd_all, h)), 0.0).astype(BF16) for h in H]
    o_state = [_dot(ksl(qd_all, h), state[h].astype(BF16)) for h in H]
    o = [_dot(att[h], v[h]) + o_state[h] for h in H]
    new_state = [state[h] * scale_col[h * GLA_HEAD_K:(h + 1) * GLA_HEAD_K] + _dot_tn(ksl(kt_all, h), v[h]) for h in H]
    for h in H:
        state_ref[h] = new_state[h]
    outs = []
    for h in H:
        ms = jnp.mean(o[h] * o[h], axis=-1, keepdims=True)
        gate = _silu(r_ref[:, h * GLA_HEAD_V:(h + 1) * GLA_HEAD_V])
        outs.append((o[h] * lax.rsqrt(ms + RMS_EPS) * nw_ref[...] * gate).astype(o_ref.dtype))
    o_ref[...] = jnp.concatenate(outs, axis=1)


def _gla(qkv, slab, gate_r, wg_pad, gate_bias, norm_w, s0, *, steps):
    v_block = 2 * GLA_K_W // GLA_V_W
    state_block = (None, GLA_HEADS, GLA_HEAD_K, GLA_HEAD_V)
    in_specs = [
        pl.BlockSpec((CHUNK, GLA_K_W), lambda s: (s, 0)),
        pl.BlockSpec((CHUNK, GLA_K_W), lambda s: (s, 1)),
        pl.BlockSpec((CHUNK, GLA_V_W), lambda s: (s, v_block)),
        pl.BlockSpec((CHUNK, LANES), lambda s: (s, SLAB_COL // LANES)),
        pl.BlockSpec((CHUNK, GLA_V_W), lambda s: (s, 0)),
        pl.BlockSpec((LANES, GLA_K_W), lambda s: (0, 0)),
        pl.BlockSpec((1, GLA_K_W), lambda s: (0, 0)),
        pl.BlockSpec((1, GLA_HEAD_V), lambda s: (0, 0)),
        pl.BlockSpec(state_block, lambda s: (steps.sample_index(s), 0, 0, 0)),
    ]
    out_specs = [
        pl.BlockSpec((CHUNK, GLA_V_W), lambda s: (s, 0)),
        pl.BlockSpec(state_block, lambda s: (steps.locate(s)[0], 0, 0, 0)),
    ]
    out_shape = [jax.ShapeDtypeStruct((steps.n_steps * CHUNK, GLA_V_W), BF16),
                 jax.ShapeDtypeStruct((steps.bp + steps.bs, GLA_HEADS, GLA_HEAD_K, GLA_HEAD_V), F32)]
    return pl.pallas_call(
        functools.partial(_gla_body, steps=steps), grid=(steps.n_steps,), in_specs=in_specs, out_specs=out_specs,
        out_shape=out_shape, compiler_params=_params("arbitrary"), name="gla_chunk",
    )(qkv, qkv, qkv, slab, gate_r, wg_pad, gate_bias, norm_w, s0)


def _gdn_body(x_ref, slab_ref, z_ref, cw_ref, alog_ref, dtb_ref, nw_ref, conv0_ref, s0_ref,
              o_ref, conv_ref, state_ref, y_ref, *, steps):
    _, chunk, is_prompt = steps.locate(pl.program_id(0))

    @pl.when((chunk == 0) & is_prompt)
    def _():
        conv_ref[...] = jnp.zeros_like(conv_ref)
        state_ref[...] = jnp.zeros_like(state_ref)

    @pl.when((chunk == 0) & jnp.logical_not(is_prompt))
    def _():
        conv_ref[...] = conv0_ref[...]
        state_ref[...] = s0_ref[...]

    x = x_ref[...]
    full = jnp.concatenate([conv_ref[...], x], axis=0)
    conv = cw_ref[GDN_CONV - 1][None] * x
    for d in range(1, GDN_CONV):
        conv = conv + cw_ref[GDN_CONV - 1 - d][None] * full[CONV_PAD_ROWS - d:CONV_PAD_ROWS - d + CHUNK]
    conv_ref[...] = x[CHUNK - CONV_PAD_ROWS:]
    y = _silu(conv)
    n_qk = 2 * GDN_HEADS
    yqk = y[:, :n_qk, :]
    sq = (yqk * yqk).reshape(CHUNK * n_qk, GDN_HEAD_K).astype(BF16)
    ssq = _dot(sq, jnp.ones((GDN_HEAD_K, GDN_HEAD_K), BF16)).reshape(CHUNK, n_qk, GDN_HEAD_K)
    n_sl = y.shape[1]
    yn = jnp.concatenate([yqk * lax.rsqrt(ssq + 1e-6), y[:, n_qk:, :]], axis=1)
    y_ref[...] = yn.reshape(CHUNK * n_sl, GDN_HEAD_K)

    row = lax.broadcasted_iota(jnp.int32, (CHUNK, CHUNK), 0)
    col = lax.broadcasted_iota(jnp.int32, (CHUNK, CHUNK), 1)
    lower = row >= col
    strict = row > col
    eye = (row == col).astype(F32)
    tril = lower.astype(F32)

    slab = slab_ref[...]
    beta_all = _sigmoid(slab)
    lane = lax.broadcasted_iota(jnp.int32, slab.shape, 1)
    is_decay = (lane >= SLAB_DECAY) & (lane < SLAB_DECAY + GDN_HEADS)
    g_all = jnp.where(is_decay, -jnp.exp(alog_ref[...]) * _softplus(slab + dtb_ref[...]), 0.0)
    gc_all = jnp.dot(tril, g_all, precision=HIGHEST, preferred_element_type=F32)
    gc_rows = gc_all.T

    H = range(GDN_HEADS)
    def head_rows(s):
        return y_ref[pl.ds(s, CHUNK, stride=n_sl), :]

    qn = [head_rows(h) * (GDN_HEAD_K ** -0.5) for h in H]
    kn = [head_rows(GDN_HEADS + h) for h in H]
    vs = [head_rows(2 * GDN_HEADS + h) for h in H]
    gcc = [gc_all[:, SLAB_DECAY + h:SLAB_DECAY + h + 1] for h in H]
    gcr = [gc_rows[SLAB_DECAY + h:SLAB_DECAY + h + 1, :] for h in H]
    beta = [beta_all[:, SLAB_BETA + h:SLAB_BETA + h + 1] for h in H]
    egc = [jnp.exp(g) for g in gcc]
    g_last = [g[CHUNK - 1:CHUNK, :] for g in gcc]
    decay = [jnp.exp(jnp.where(lower, gcc[h] - gcr[h], -jnp.inf)) for h in H]
    kb = [k.astype(BF16) for k in kn]
    qb = [q.astype(BF16) for q in qn]
    state = [state_ref[h] for h in H]
    sb = [s.astype(BF16) for s in state]

    a = [_dot_nt(kb[h], kb[h]) * jnp.where(strict, decay[h], 0.0) * beta[h] for h in H]
    qk = [(_dot_nt(qb[h], kb[h]) * decay[h]).astype(BF16) for h in H]
    o_state = [_dot((qn[h] * egc[h]).astype(BF16), sb[h]) for h in H]
    t = [eye - a[h] for h in H]
    apb = [x.astype(BF16) for x in a]
    for _ in range(5):
        apb = [_dot(x, x).astype(BF16) for x in apb]
        t = [t[h] + _dot(t[h].astype(BF16), apb[h]) for h in H]
    rhs = [jnp.concatenate([vs[h] * beta[h], kn[h] * (beta[h] * egc[h])], axis=1).astype(BF16) for h in H]
    uw = [_dot(t[h].astype(BF16), rhs[h]) for h in H]
    v_new = [(uw[h][:, :GDN_HEAD_V] - _dot(uw[h][:, GDN_HEAD_V:].astype(BF16), sb[h])).astype(BF16) for h in H]
    o = [o_state[h] + _dot(qk[h], v_new[h]) for h in H]
    k_tail = [(kn[h] * jnp.exp(g_last[h] - gcc[h])).astype(BF16) for h in H]
    new_state = [state[h] * jnp.exp(g_last[h]) + _dot_tn(k_tail[h], v_new[h]) for h in H]
    for h in H:
        state_ref[h] = new_state[h]
    outs = []
    for h in H:
        ms = jnp.mean(o[h] * o[h], axis=-1, keepdims=True)
        zh = z_ref[:, h * GDN_HEAD_V:(h + 1) * GDN_HEAD_V]
        outs.append((o[h] * lax.rsqrt(ms + RMS_EPS) * nw_ref[...] * _silu(zh)).astype(o_ref.dtype))
    o_ref[...] = jnp.concatenate(outs, axis=1)


def _gdn(qkv, slab, z, conv_w, alog_row, dtb_row, norm_w, conv0, s0, *, steps):
    n_sl = GDN_CONV_CH // LANES
    n_seq = steps.bp + steps.bs
    conv_block = (None, CONV_PAD_ROWS, n_sl, LANES)
    state_block = (None, GDN_HEADS, GDN_HEAD_K, GDN_HEAD_V)
    seq_of = lambda s: steps.locate(s)[0]
    in_specs = [
        pl.BlockSpec((CHUNK, n_sl, LANES), lambda s: (s, 0, 0)),
        pl.BlockSpec((CHUNK, LANES), lambda s: (s, SLAB_COL // LANES)),
        pl.BlockSpec((CHUNK, GDN_V_W), lambda s: (s, 0)),
        pl.BlockSpec((GDN_CONV, n_sl, LANES), lambda s: (0, 0, 0)),
        pl.BlockSpec((1, LANES), lambda s: (0, 0)),
        pl.BlockSpec((1, LANES), lambda s: (0, 0)),
        pl.BlockSpec((1, GDN_HEAD_V), lambda s: (0, 0)),
        pl.BlockSpec(conv_block, lambda s: (steps.sample_index(s), 0, 0, 0)),
        pl.BlockSpec(state_block, lambda s: (steps.sample_index(s), 0, 0, 0)),
    ]
    out_specs = [
        pl.BlockSpec((CHUNK, GDN_V_W), lambda s: (s, 0)),
        pl.BlockSpec(conv_block, lambda s: (seq_of(s), 0, 0, 0)),
        pl.BlockSpec(state_block, lambda s: (seq_of(s), 0, 0, 0)),
    ]
    out_shape = [jax.ShapeDtypeStruct((steps.n_steps * CHUNK, GDN_V_W), BF16),
                 jax.ShapeDtypeStruct((n_seq, CONV_PAD_ROWS, n_sl, LANES), F32),
                 jax.ShapeDtypeStruct((n_seq, GDN_HEADS, GDN_HEAD_K, GDN_HEAD_V), F32)]
    o, conv, state = pl.pallas_call(
        functools.partial(_gdn_body, steps=steps), grid=(steps.n_steps,), in_specs=in_specs, out_specs=out_specs,
        out_shape=out_shape, scratch_shapes=[pltpu.VMEM((CHUNK * n_sl, LANES), F32)],
        compiler_params=_params("arbitrary"), name="gdn_chunk",
    )(qkv.reshape(-1, n_sl, LANES), slab, z, conv_w.reshape(GDN_CONV, n_sl, LANES), alog_row, dtb_row, norm_w,
      conv0.reshape(steps.bs, CONV_PAD_ROWS, n_sl, LANES), s0)
    return o, conv.reshape(n_seq, CONV_PAD_ROWS, GDN_CONV_CH), state


KV_SLAB_W = 1024
SLAB_COL = 2 * SWA_KV_W


def _split_w_in(w):
    wt = jnp.swapaxes(w, 1, 2)
    edges = [0]
    for s in IN_SIZES:
        edges.append(edges[-1] + s)
    rows = lambda a, b: wt[:, edges[a]:edges[b], :].astype(BF16)
    zeros = lambda n: jnp.zeros((w.shape[0], n, w.shape[1]), BF16)
    slab = [rows(6, 7), zeros(SLAB_BETA - SLAB_GLR - GLA_RANK), rows(9, 10),
            zeros(SLAB_DECAY - SLAB_BETA - GDN_HEADS), rows(10, 11)]
    used = SLAB_COL + SLAB_DECAY + GDN_HEADS
    kv_slab = jnp.concatenate([rows(1, 3)] + slab + [zeros(KV_SLAB_W - used)], axis=1)
    return dict(kv_slab=kv_slab, swa_q=rows(0, 1), gla_qkv=rows(3, 6), gla_r=rows(7, 8), gdn_qkv=rows(8, 9),
                gdn_z=rows(11, 12), gates=rows(12, 13))


def _swa_tables(sinks):
    tok = jnp.arange(CHUNK)
    key = jnp.arange(SWA_SEG)
    dist = jnp.abs((tok + SWA_WINDOW)[:, None] - key[None, :]).astype(F32)
    head = (jnp.arange(SWA_KV_HEADS)[:, None, None] * SWA_GROUP + jnp.arange(SWA_PAIRS)[None, :, None] * 2
            + jnp.arange(2)[None, None, :])
    slopes = 2.0 ** (-8.0 * (head + 1).astype(F32) / SWA_HEADS)
    bias = slopes[:, :, None, :, None] * dist[None, None, :, None, :]
    bias = jnp.where(key < SWA_KEYS, bias, SWA_PAD_BIAS)
    bias = bias.reshape(SWA_KV_HEADS, SWA_PAIRS * CHUNK, 2 * SWA_SEG)
    sink = jnp.repeat(sinks.astype(F32)[head], SWA_HEAD_DIM, axis=-1)
    sink = jnp.broadcast_to(sink[:, :, None, :], (SWA_KV_HEADS, SWA_PAIRS, CHUNK, LANES))
    return bias, sink.reshape(SWA_KV_HEADS, SWA_PAIRS * CHUNK, LANES)


def _key_rows(new, cache, *, batch, seq):
    t = new.reshape(batch, seq, SWA_KV_HEADS, SWA_HEAD_DIM)
    if cache is None:
        return jnp.pad(t, ((0, 0), (SWA_WINDOW, 0), (0, 0), (0, 0)))
    return jnp.concatenate([cache, t], axis=1)


def _pad_rows(t, n_rows):
    return jnp.pad(t, ((0, 0), (n_rows - t.shape[1], 0), (0, 0)))


def _layer(x, xb, pe_b, W, WB, layer, last, cache_k, cache_v, gla_s0, gdn_s0, conv_s0, *, bp, sp, bs, ss):
    n_prompt = bp * sp
    steps = _Steps(bp=bp, npc=sp // CHUNK, bs=bs, nsc=ss // CHUNK)
    win = WB['w_in']
    ident = _ep_identity
    proj = lambda group: _matmul(xb, win[group], layer, [F32], ident, w_rows=True, name="proj_" + group)[0]
    kv_slab = proj('kv_slab')
    swa_q = proj('swa_q')
    gla_qkv = proj('gla_qkv')
    gla_r = proj('gla_r')
    gdn_qkv = proj('gdn_qkv')
    gdn_z = proj('gdn_z')

    bias, sink = _swa_tables(W['swa_sinks'])
    new_kv, kv_all = [], []
    for col, cache in ((0, cache_k), (1, cache_v)):
        t = kv_slab[:, col * SWA_KV_W:(col + 1) * SWA_KV_W]
        rows_p = _key_rows(t[:n_prompt], None, batch=bp, seq=sp)
        rows_s = _key_rows(t[n_prompt:], cache, batch=bs, seq=ss)
        new_kv.append((rows_p[:, sp:], rows_s[:, ss:]))
        flat = jnp.concatenate([rows_p.reshape(-1, SWA_KV_HEADS, SWA_HEAD_DIM),
                                rows_s.reshape(-1, SWA_KV_HEADS, SWA_HEAD_DIM)], axis=0)
        flat = jnp.transpose(flat, (1, 0, 2)).astype(BF16)
        kv_all.append(jnp.concatenate([flat, flat], axis=-1))
    o_a = _swa(swa_q, kv_all[0], kv_all[1], bias, sink, steps=steps)

    wg_pad = jnp.zeros((LANES, GLA_K_W), F32).at[SLAB_GLR:SLAB_GLR + GLA_RANK].set(W['gla_w_gate2'])
    gbias = W['gla_gate_bias'].reshape(1, GLA_K_W)
    gnorm = W['gla_norm_w'].reshape(1, GLA_HEAD_V)
    o_b, gla_state = _gla(gla_qkv, kv_slab, gla_r, wg_pad, gbias, gnorm, gla_s0, steps=steps)

    alog_row = jnp.zeros((1, LANES), F32).at[0, SLAB_DECAY:SLAB_DECAY + GDN_HEADS].set(W['gdn_a_log'])
    dtb_row = jnp.zeros((1, LANES), F32).at[0, SLAB_DECAY:SLAB_DECAY + GDN_HEADS].set(W['gdn_dt_bias'])
    dnorm = W['gdn_norm_w'].reshape(1, GDN_HEAD_V)
    o_c, conv_state, gdn_state = _gdn(gdn_qkv, kv_slab, gdn_z, W['gdn_conv_w'], alog_row, dtb_row, dnorm,
                                      _pad_rows(conv_s0, CONV_PAD_ROWS), gdn_s0, steps=steps)
    conv_state = conv_state[:, CONV_PAD_ROWS - (GDN_CONV - 1):]

    merged = _merge(xb, win['gates'], jnp.stack([o_a, o_b, o_c]), WB['w_br'], layer)
    tm = _row_tile(x.shape[0], 512)
    tn = 512
    x, xb = _matmul_ln(merged, WB['w_out'], layer, W['ln1_g'], W['ln1_b'], _ep_residual,
                       [(x, *_tile_spec(tm, tn))], tm=tm, tn=tn, name="out_proj_ln")
    hid = _matmul(xb, WB['w_up'], layer, [BF16], _ep_relu_sq, name="ffn_up")[0]
    tm_down = _row_tile(x.shape[0], 1024)
    z = _matmul(hid, WB['w_down'], layer, [F32], _ep_residual, extras=[(x, *_tile_spec(tm_down, 1024))],
                tm=tm_down, tk=2048, name="ffn_down")[0]
    x, xb = _layer_norm(z, W['ln2_g'], W['ln2_b'])
    pe_extras = [(x, *_tile_spec(tm, tn)),
                 (pe_b, (tm, PE_DIM), lambda i, j, k: (i, 0)),
                 (WB['pe_w_proj'], (None, PE_DIM, tn), lambda i, j, k: (layer, 0, j))]
    pe_gate = functools.partial(_matmul_ln, xb, WB['pe_w_gate'], layer, W['ln3_g'], W['ln3_b'], _ep_pe, pe_extras,
                                tm=tm, tn=tn, name="pe_gate_ln")
    if last:
        x, xb = pe_gate(rows=(0, n_prompt))[0], pe_gate(rows=(n_prompt, x.shape[0] - n_prompt))[0]
    else:
        x, xb = pe_gate()
    states_p = (new_kv[0][0], new_kv[1][0], gla_state[:bp], gdn_state[:bp], conv_state[:bp])
    states_s = (new_kv[0][1], new_kv[1][1], gla_state[bp:], gdn_state[bp:], conv_state[bp:])
    return x, xb, states_p, states_s


def kernel(x_prompt, x_sample, cache_swa_k, cache_swa_v, state_gla, state_gdn, state_gdn_conv, p_prompt, p_sample, w_in, swa_sinks, gla_w_gate2, gla_gate_bias, gla_norm_w, gdn_conv_w, gdn_a_log, gdn_dt_bias, gdn_norm_w, w_br_swa, w_br_gla, w_br_gdn, w_out, ln1_g, ln1_b, w_up, w_down, ln2_g, ln2_b, pe_w_gate, pe_w_proj, ln3_g, ln3_b):
    bp, sp, d = x_prompt.shape
    bs, ss, _ = x_sample.shape
    n_prompt = bp * sp
    x = jnp.concatenate([x_prompt.reshape(n_prompt, d), x_sample.reshape(bs * ss, d)], axis=0)
    xb = x.astype(BF16)
    WB = {'w_in': _split_w_in(w_in), 'w_br': jnp.stack([w_br_swa, w_br_gla, w_br_gdn], axis=1).astype(BF16),
          'w_out': w_out.astype(BF16), 'w_up': w_up.astype(BF16), 'w_down': w_down.astype(BF16),
          'pe_w_gate': pe_w_gate.astype(BF16), 'pe_w_proj': pe_w_proj.astype(BF16)}
    st_p, st_s = [], []
    for l in range(DEPTH):
        W = {'swa_sinks': swa_sinks[l], 'gla_w_gate2': gla_w_gate2[l],
             'gla_gate_bias': gla_gate_bias[l], 'gla_norm_w': gla_norm_w[l], 'gdn_conv_w': gdn_conv_w[l],
             'gdn_a_log': gdn_a_log[l], 'gdn_dt_bias': gdn_dt_bias[l], 'gdn_norm_w': gdn_norm_w[l],
             'ln1_g': ln1_g[l], 'ln1_b': ln1_b[l], 'ln2_g': ln2_g[l], 'ln2_b': ln2_b[l],
             'ln3_g': ln3_g[l], 'ln3_b': ln3_b[l]}
        pe_b = jnp.concatenate([p_prompt[l].reshape(n_prompt, PE_DIM), p_sample[l].reshape(bs * ss, PE_DIM)],
                               axis=0).astype(BF16)
        x, xb, sp_l, ss_l = _layer(x, xb, pe_b, W, WB, l, l == DEPTH - 1, cache_swa_k[l], cache_swa_v[l],
                                   state_gla[l], state_gdn[l], state_gdn_conv[l], bp=bp, sp=sp, bs=bs, ss=ss)
        st_p.append(sp_l)
        st_s.append(ss_l)
    y_prompt = x.reshape(bp, sp, d)
    y_sample = xb.reshape(bs, ss, d)
    stack = lambda sts, i: jnp.stack([s[i] for s in sts])
    return (y_prompt, y_sample,
            stack(st_p, 0), stack(st_p, 1), stack(st_p, 2), stack(st_p, 3), stack(st_p, 4),
            stack(st_s, 0), stack(st_s, 1), stack(st_s, 2), stack(st_s, 3), stack(st_s, 4))
```

```python
import functools
from typing import NamedTuple

import jax
import jax.numpy as jnp
from jax import lax
from jax.experimental import pallas as pl
from jax.experimental.pallas import tpu as pltpu

F32 = jnp.float32
BF16 = jnp.bfloat16
HIGHEST = lax.Precision.HIGHEST

D_MODEL = 4096
DEPTH = 2
CHUNK = 64
PE_DIM = 256
D_FF = 4 * D_MODEL
LN_EPS = 1e-5
RMS_EPS = 1e-6
NEG_INF = -1e30

SWA_HEADS = 32
SWA_KV_HEADS = 4
SWA_GROUP = SWA_HEADS // SWA_KV_HEADS
SWA_HEAD_DIM = 64
SWA_WINDOW = 128
SWA_KEYS = SWA_WINDOW + CHUNK
GLA_HEADS = 4
GLA_HEAD_K = 256
GLA_HEAD_V = 512
GLA_RANK = 16
GLA_TAU = 16.0
GDN_HEADS = 16
GDN_HEAD_K = 128
GDN_HEAD_V = 128
GDN_CONV = 4

SWA_Q_W = SWA_HEADS * SWA_HEAD_DIM
SWA_KV_W = SWA_KV_HEADS * SWA_HEAD_DIM
GLA_K_W = GLA_HEADS * GLA_HEAD_K
GLA_V_W = GLA_HEADS * GLA_HEAD_V
GDN_K_W = GDN_HEADS * GDN_HEAD_K
GDN_V_W = GDN_HEADS * GDN_HEAD_V
GDN_CONV_CH = 2 * GDN_K_W + GDN_V_W
N_BRANCH = 3
IN_SIZES = (SWA_Q_W, SWA_KV_W, SWA_KV_W, GLA_K_W, GLA_K_W, GLA_V_W, GLA_RANK, GLA_V_W,
            GDN_CONV_CH, GDN_HEADS, GDN_HEADS, GDN_V_W, N_BRANCH * D_MODEL)

DN_ALPHA = (2 * DEPTH) ** 0.25

LANES = 128
SUBLANES = 8
VMEM_LIMIT_BYTES = 56 * 1024 * 1024
SLAB_GLR = 0
SLAB_BETA = 16
SLAB_DECAY = 32
CONV_PAD_ROWS = SUBLANES
SWA_PAIRS = SWA_GROUP * SWA_HEAD_DIM // LANES
SWA_SEG = 256
SWA_PAD_BIAS = 1e30


def _params(*sem):
    return pltpu.CompilerParams(dimension_semantics=sem, vmem_limit_bytes=VMEM_LIMIT_BYTES)


def _row_tile(m, want):
    t = want
    while m % t:
        t //= 2
    assert t >= SUBLANES
    return t


class _Steps(NamedTuple):
    bp: int
    npc: int
    bs: int
    nsc: int

    @property
    def n_prompt(self):
        return self.bp * self.npc

    @property
    def n_steps(self):
        return self.bp * self.npc + self.bs * self.nsc

    def locate(self, s):
        is_prompt = s < self.n_prompt
        r = jnp.maximum(s - self.n_prompt, 0)
        seq = jnp.where(is_prompt, s // self.npc, self.bp + r // self.nsc)
        chunk = jnp.where(is_prompt, s % self.npc, r % self.nsc)
        return seq, chunk, is_prompt

    def sample_index(self, s):
        seq, _, _ = self.locate(s)
        return jnp.maximum(seq - self.bp, 0)


def _dot(a, b):
    return jnp.dot(a, b, preferred_element_type=F32)


def _dot_nt(a, b):
    return lax.dot_general(a, b, (((1,), (1,)), ((), ())), preferred_element_type=F32)


def _dot_tn(a, b):
    return lax.dot_general(a, b, (((0,), (0,)), ((), ())), preferred_element_type=F32)


def _sigmoid(x):
    return 0.5 * jnp.tanh(0.5 * x) + 0.5


def _silu(x):
    return x * _sigmoid(x)


def _softplus(x):
    return jnp.maximum(x, 0.0) + jnp.log(1.0 + jnp.exp(-jnp.abs(x)))


def _mm_body(*refs, nk, n_extra, n_out, epilogue, w_rows):
    a_ref, w_ref = refs[0], refs[1]
    extra = refs[2:2 + n_extra]
    outs = refs[2 + n_extra:2 + n_extra + n_out]
    dot = _dot_nt if w_rows else _dot

    def finish(acc):
        vals = epilogue(acc, *extra)
        for o, v in zip(outs, vals):
            o[...] = v.astype(o.dtype)

    if nk == 1:
        finish(dot(a_ref[...], w_ref[...]))
    else:
        acc_ref = refs[-1]
        k = pl.program_id(2)

        @pl.when(k == 0)
        def _():
            acc_ref[...] = jnp.zeros_like(acc_ref)

        acc_ref[...] += dot(a_ref[...], w_ref[...])

        @pl.when(k == nk - 1)
        def _():
            finish(acc_ref[...])


def _matmul(a, w, layer, out_dtypes, epilogue, extras=(), *, w_rows=False, tm=1024, tn=1024, tk=None, name):
    M, K = a.shape
    N = w.shape[1] if w_rows else w.shape[2]
    tm = _row_tile(M, tm)
    tn = min(tn, N)
    tk = K if tk is None else tk
    assert M % tm == 0 and N % tn == 0 and K % tk == 0
    nk = K // tk
    w_spec = (pl.BlockSpec((None, tn, tk), lambda i, j, k: (layer, j, k)) if w_rows
              else pl.BlockSpec((None, tk, tn), lambda i, j, k: (layer, k, j)))
    in_specs = [pl.BlockSpec((tm, tk), lambda i, j, k: (i, k)), w_spec]
    args = [a, w]
    for arr, block, imap in extras:
        in_specs.append(pl.BlockSpec(block, imap))
        args.append(arr)
    out_specs = [pl.BlockSpec((tm, tn), lambda i, j, k: (i, j)) for _ in out_dtypes]
    out_shape = [jax.ShapeDtypeStruct((M, N), dt) for dt in out_dtypes]
    scratch = [pltpu.VMEM((tm, tn), F32)] if nk > 1 else []
    body = functools.partial(_mm_body, nk=nk, n_extra=len(extras), n_out=len(out_dtypes), epilogue=epilogue,
                             w_rows=w_rows)
    return pl.pallas_call(
        body, grid=(M // tm, N // tn, nk), in_specs=in_specs, out_specs=out_specs, out_shape=out_shape,
        scratch_shapes=scratch, compiler_params=_params("parallel", "parallel", "arbitrary"), name=name,
    )(*args)


def _tile_spec(tm, tn):
    return (tm, tn), (lambda i, j, k: (i, j))


def _ep_identity(acc):
    return (acc,)


def _ep_relu_sq(acc):
    r = jnp.maximum(acc, 0.0)
    return (r * r,)


def _ep_residual(acc, x_ref):
    return (DN_ALPHA * x_ref[...] + acc,)


def _ep_pe(acc, x_ref, pe_ref, wp_ref):
    proj = _dot(pe_ref[...], wp_ref[...])
    return (DN_ALPHA * x_ref[...] + _sigmoid(acc) * proj,)


def _merge_body(x_ref, wg_ref, o_ref, w_ref, out_ref, acc_ref):
    b = pl.program_id(2)
    gate = _sigmoid(_dot_nt(x_ref[...], wg_ref[...]))
    contrib = gate * _dot(o_ref[...], w_ref[...])

    @pl.when(b == 0)
    def _():
        acc_ref[...] = contrib

    @pl.when(b > 0)
    def _():
        acc_ref[...] += contrib

    @pl.when(b == N_BRANCH - 1)
    def _():
        out_ref[...] = acc_ref[...].astype(out_ref.dtype)


def _merge(xb, w_gates, o_stack, w_stack, layer, *, tm=1024, tn=512):
    _, M, K = o_stack.shape
    D = xb.shape[1]
    N = w_stack.shape[3]
    tm = _row_tile(M, tm)
    nj = N // tn
    return pl.pallas_call(
        _merge_body, grid=(M // tm, nj, N_BRANCH),
        in_specs=[pl.BlockSpec((tm, D), lambda i, j, b: (i, 0)),
                  pl.BlockSpec((None, tn, D), lambda i, j, b: (layer, b * nj + j, 0)),
                  pl.BlockSpec((None, tm, K), lambda i, j, b: (b, i, 0)),
                  pl.BlockSpec((None, None, K, tn), lambda i, j, b: (layer, b, 0, j))],
        out_specs=pl.BlockSpec((tm, tn), lambda i, j, b: (i, j)),
        out_shape=jax.ShapeDtypeStruct((M, N), BF16),
        scratch_shapes=[pltpu.VMEM((tm, tn), F32)],
        compiler_params=_params("parallel", "parallel", "arbitrary"), name="merge_branches",
    )(xb, w_gates, o_stack, w_stack)


def _ln_rows(z, g, b):
    mu = jnp.mean(z, axis=-1, keepdims=True)
    d = z - mu
    var = jnp.mean(d * d, axis=-1, keepdims=True)
    return d * lax.rsqrt(var + LN_EPS) * g + b


def _ln_body(z_ref, g_ref, b_ref, x_ref, xb_ref):
    y = _ln_rows(z_ref[...], g_ref[...], b_ref[...])
    x_ref[...] = y
    xb_ref[...] = y.astype(BF16)


def _layer_norm(z, g, b, *, tm=256):
    M, D = z.shape
    tm = _row_tile(M, tm)
    row = pl.BlockSpec((tm, D), lambda i: (i, 0))
    vec = pl.BlockSpec((1, D), lambda i: (0, 0))
    return pl.pallas_call(
        _ln_body, grid=(M // tm,), in_specs=[row, vec, vec], out_specs=[row, row],
        out_shape=[jax.ShapeDtypeStruct((M, D), F32), jax.ShapeDtypeStruct((M, D), BF16)],
        compiler_params=_params("parallel"), name="layer_norm",
    )(z, g.reshape(1, D), b.reshape(1, D))


def _mm_ln_body(*refs, nj, nk, tn, n_extra, epilogue):
    a_ref, w_ref = refs[0], refs[1]
    extra = refs[2:2 + n_extra]
    g_ref, b_ref, x_ref, xb_ref = refs[2 + n_extra:6 + n_extra]
    j = pl.program_id(1)

    def finish(acc):
        z = epilogue(acc, *extra)[0]
        for jj in range(nj):
            @pl.when(j == jj)
            def _(jj=jj):
                x_ref[:, jj * tn:(jj + 1) * tn] = z

        @pl.when(j == nj - 1)
        def _():
            y = _ln_rows(x_ref[...], g_ref[...], b_ref[...])
            x_ref[...] = y
            xb_ref[...] = y.astype(BF16)

    if nk == 1:
        finish(_dot(a_ref[...], w_ref[...]))
    else:
        acc_ref = refs[-1]
        k = pl.program_id(2)

        @pl.when(k == 0)
        def _():
            acc_ref[...] = jnp.zeros_like(acc_ref)

        acc_ref[...] += _dot(a_ref[...], w_ref[...])

        @pl.when(k == nk - 1)
        def _():
            finish(acc_ref[...])


def _matmul_ln(a, w, layer, g, b, epilogue, extras, *, tm=512, tn=512, tk=None, rows=None, name):
    M, K = a.shape
    N = w.shape[2]
    row0, n_rows = (0, M) if rows is None else rows
    tm = _row_tile(n_rows, tm)
    tk = K if tk is None else tk
    assert N % tn == 0 and K % tk == 0 and row0 % tm == 0
    nj, nk = N // tn, K // tk
    rb0 = row0 // tm
    in_specs = [pl.BlockSpec((tm, tk), lambda i, j, k: (rb0 + i, k)),
                pl.BlockSpec((None, tk, tn), lambda i, j, k: (layer, k, j))]
    args = [a, w]
    for arr, block, imap in extras:
        in_specs.append(pl.BlockSpec(block, lambda i, j, k, imap=imap: imap(rb0 + i, j, k)))
        args.append(arr)
    vec = pl.BlockSpec((1, N), lambda i, j, k: (0, 0))
    row = pl.BlockSpec((tm, N), lambda i, j, k: (i, 0))
    scratch = [pltpu.VMEM((tm, tn), F32)] if nk > 1 else []
    body = functools.partial(_mm_ln_body, nj=nj, nk=nk, tn=tn, n_extra=len(extras), epilogue=epilogue)
    return pl.pallas_call(
        body, grid=(n_rows // tm, nj, nk), in_specs=in_specs + [vec, vec], out_specs=[row, row],
        out_shape=[jax.ShapeDtypeStruct((n_rows, N), F32), jax.ShapeDtypeStruct((n_rows, N), BF16)],
        scratch_shapes=scratch, compiler_params=_params("parallel", "arbitrary", "arbitrary"), name=name,
    )(*args, g.reshape(1, N), b.reshape(1, N))


def _swa_body(*refs, steps, n_sub):
    q_ref, bias_ref, sink_ref, o_ref = refs[0], refs[-3], refs[-2], refs[-1]
    n_win = SWA_KEYS // CHUNK
    k_refs = [refs[1 + 2 * n_win * c:1 + 2 * n_win * c + n_win] for c in range(n_sub)]
    v_refs = [refs[1 + 2 * n_win * c + n_win:1 + 2 * n_win * (c + 1)] for c in range(n_sub)]
    where = [steps.locate(pl.program_id(0) * n_sub + c) for c in range(n_sub)]
    J = [(c, j) for c in range(n_sub) for j in range(SWA_KV_HEADS)]
    C = range(len(J))
    P = range(SWA_PAIRS)
    low = lax.broadcasted_iota(jnp.int32, (SWA_KEYS, LANES), 1) < SWA_HEAD_DIM
    zero = jnp.zeros((SWA_KEYS, LANES), BF16)
    seg_pad = jnp.zeros((SWA_SEG - SWA_KEYS, LANES), BF16)

    def pair_operand(t):
        return jnp.concatenate([jnp.where(low, t, zero), seg_pad, jnp.where(low, zero, t), seg_pad], axis=0)

    def q_tile(c, j, p):
        lane0 = (j * SWA_PAIRS + p) * LANES
        return q_ref[c * CHUNK:(c + 1) * CHUNK, lane0:lane0 + LANES]

    q = [(jnp.concatenate([q_tile(c, j, p) for p in P], axis=0) * (SWA_HEAD_DIM ** -0.5)).astype(BF16) for c, j in J]
    k = [pair_operand(jnp.concatenate([r[j] for r in k_refs[c]], axis=0)) for c, j in J]
    v = [pair_operand(jnp.concatenate([r[j] for r in v_refs[c]], axis=0)) for c, j in J]
    s = [_dot_nt(q[i], k[i]) - bias_ref[J[i][1]] for i in C]
    key = lax.broadcasted_iota(jnp.int32, s[0].shape, 1) & (SWA_SEG - 1)
    valid = [key + jnp.where(is_prompt, chunk * CHUNK, SWA_WINDOW) >= SWA_WINDOW for _, chunk, is_prompt in where]
    s = [jnp.where(valid[J[i][0]], s[i], NEG_INF) for i in C]
    halves = [(x[:, :SWA_SEG], x[:, SWA_SEG:]) for x in s]
    sinks = [(sink_ref[j][:, 0:1], sink_ref[j][:, SWA_HEAD_DIM:SWA_HEAD_DIM + 1]) for _, j in J]
    m = [[jnp.maximum(jnp.max(halves[i][e], axis=-1, keepdims=True), sinks[i][e]) for e in (0, 1)] for i in C]
    p = [[jnp.exp(halves[i][e] - m[i][e]) for e in (0, 1)] for i in C]
    denom = [[jnp.sum(p[i][e], axis=-1, keepdims=True) + jnp.exp(sinks[i][e] - m[i][e]) for e in (0, 1)] for i in C]
    o = [_dot(jnp.concatenate(p[i], axis=1).astype(BF16), v[i]) for i in C]
    even = lax.broadcasted_iota(jnp.int32, o[0].shape, 1) < SWA_HEAD_DIM
    o = [o[i] / jnp.where(even, denom[i][0], denom[i][1]) for i in C]
    for c in range(n_sub):
        tiles = [o[c * SWA_KV_HEADS + j][p * CHUNK:(p + 1) * CHUNK] for j in range(SWA_KV_HEADS) for p in P]
        o_ref[c * CHUNK:(c + 1) * CHUNK, :] = jnp.concatenate(tiles, axis=1).astype(o_ref.dtype)


def _swa(q, k_all, v_all, bias, sink, *, steps):
    back = SWA_WINDOW // CHUNK
    n_sub = next(n for n in (4, 2, 1) if steps.n_steps % n == 0)

    def first_key_chunk(s):
        seq, chunk, is_prompt = steps.locate(s)
        prompt_chunks = steps.bp * (steps.npc + back)
        return jnp.where(is_prompt, seq * (steps.npc + back), prompt_chunks + (seq - steps.bp) * (steps.nsc + back)) + chunk

    def win(c, w):
        return pl.BlockSpec((SWA_KV_HEADS, CHUNK, LANES), lambda s: (0, first_key_chunk(s * n_sub + c) + w, 0))

    rows = CHUNK * SWA_PAIRS
    q_spec = pl.BlockSpec((n_sub * CHUNK, SWA_Q_W), lambda s: (s, 0))
    bias_spec = pl.BlockSpec((SWA_KV_HEADS, rows, 2 * SWA_SEG), lambda s: (0, 0, 0))
    sink_spec = pl.BlockSpec((SWA_KV_HEADS, rows, LANES), lambda s: (0, 0, 0))
    n_win = SWA_KEYS // CHUNK
    win_specs, win_args = [], []
    for c in range(n_sub):
        for arr in (k_all, v_all):
            win_specs += [win(c, w) for w in range(n_win)]
            win_args += [arr] * n_win
    return pl.pallas_call(
        functools.partial(_swa_body, steps=steps, n_sub=n_sub), grid=(steps.n_steps // n_sub,),
        in_specs=[q_spec] + win_specs + [bias_spec, sink_spec],
        out_specs=q_spec, out_shape=jax.ShapeDtypeStruct(q.shape, BF16),
        compiler_params=_params("parallel"), name="swa_chunk",
    )(q, *win_args, bias, sink)


def _gla_body(q_ref, k_ref, v_ref, slab_ref, r_ref, wg_ref, gb_ref, nw_ref, s0_ref, o_ref, state_ref, *, steps):
    _, chunk, is_prompt = steps.locate(pl.program_id(0))

    @pl.when((chunk == 0) & is_prompt)
    def _():
        state_ref[...] = jnp.zeros_like(state_ref)

    @pl.when((chunk == 0) & jnp.logical_not(is_prompt))
    def _():
        state_ref[...] = s0_ref[...]

    row = lax.broadcasted_iota(jnp.int32, (CHUNK, CHUNK), 0)
    col = lax.broadcasted_iota(jnp.int32, (CHUNK, CHUNK), 1)
    causal = row >= col
    tril = causal.astype(F32)

    ga = jnp.dot(slab_ref[...], wg_ref[...], precision=HIGHEST, preferred_element_type=F32) + gb_ref[...]
    log_a = (jnp.minimum(ga, 0.0) - jnp.log(1.0 + jnp.exp(-jnp.abs(ga)))) * (1.0 / GLA_TAU)
    b = jnp.dot(tril, log_a, precision=HIGHEST, preferred_element_type=F32)
    b_last = b[CHUNK - 1:CHUNK, :]
    ones = jnp.ones((CHUNK, LANES), F32)
    b_last_col = lax.dot_general(log_a, ones, (((0,), (0,)), ((), ())), precision=HIGHEST,
                                 preferred_element_type=F32)[:, :1]

    qd_all = (q_ref[...] * (GLA_HEAD_K ** -0.5) * jnp.exp(b)).astype(BF16)
    k_all = k_ref[...]
    kd_all = (k_all * jnp.exp(-b)).astype(BF16)
    kt_all = (k_all * jnp.exp(b_last - b)).astype(BF16)
    scale_col = jnp.exp(b_last_col)

    H = range(GLA_HEADS)
    ksl = lambda t, h: t[:, h * GLA_HEAD_K:(h + 1) * GLA_HEAD_K]
    v = [v_ref[:, h * GLA_HEAD_V:(h + 1) * GLA_HEAD_V].astype(BF16) for h in H]
    state = [state_ref[h] for h in H]
    att = [jnp.where(causal, _dot_nt(ksl(qd_all, h), ksl(kd_all, h)), 0.0).astype(BF16) for h in H]
    o_state = [_dot(ksl(qd_all, h), state[h].astype(BF16)) for h in H]
    o = [_dot(att[h], v[h]) + o_state[h] for h in H]
    new_state = [state[h] * scale_col[h * GLA_HEAD_K:(h + 1) * GLA_HEAD_K] + _dot_tn(ksl(kt_all, h), v[h]) for h in H]
    for h in H:
        state_ref[h] = new_state[h]
    outs = []
    for h in H:
        ms = jnp.mean(o[h] * o[h], axis=-1, keepdims=True)
        gate = _silu(r_ref[:, h * GLA_HEAD_V:(h + 1) * GLA_HEAD_V])
        outs.append((o[h] * lax.rsqrt(ms + RMS_EPS) * nw_ref[...] * gate).astype(o_ref.dtype))
    o_ref[...] = jnp.concatenate(outs, axis=1)


def _gla(qkv, slab, gate_r, wg_pad, gate_bias, norm_w, s0, *, steps):
    v_block = 2 * GLA_K_W // GLA_V_W
    state_block = (None, GLA_HEADS, GLA_HEAD_K, GLA_HEAD_V)
    in_specs = [
        pl.BlockSpec((CHUNK, GLA_K_W), lambda s: (s, 0)),
        pl.BlockSpec((CHUNK, GLA_K_W), lambda s: (s, 1)),
        pl.BlockSpec((CHUNK, GLA_V_W), lambda s: (s, v_block)),
        pl.BlockSpec((CHUNK, LANES), lambda s: (s, SLAB_COL // LANES)),
        pl.BlockSpec((CHUNK, GLA_V_W), lambda s: (s, 0)),
        pl.BlockSpec((LANES, GLA_K_W), lambda s: (0, 0)),
        pl.BlockSpec((1, GLA_K_W), lambda s: (0, 0)),
        pl.BlockSpec((1, GLA_HEAD_V), lambda s: (0, 0)),
        pl.BlockSpec(state_block, lambda s: (steps.sample_index(s), 0, 0, 0)),
    ]
    out_specs = [
        pl.BlockSpec((CHUNK, GLA_V_W), lambda s: (s, 0)),
        pl.BlockSpec(state_block, lambda s: (steps.locate(s)[0], 0, 0, 0)),
    ]
    out_shape = [jax.ShapeDtypeStruct((steps.n_steps * CHUNK, GLA_V_W), BF16),
                 jax.ShapeDtypeStruct((steps.bp + steps.bs, GLA_HEADS, GLA_HEAD_K, GLA_HEAD_V), F32)]
    return pl.pallas_call(
        functools.partial(_gla_body, steps=steps), grid=(steps.n_steps,), in_specs=in_specs, out_specs=out_specs,
        out_shape=out_shape, compiler_params=_params("arbitrary"), name="gla_chunk",
    )(qkv, qkv, qkv, slab, gate_r, wg_pad, gate_bias, norm_w, s0)


def _gdn_body(x_ref, slab_ref, z_ref, cw_ref, alog_ref, dtb_ref, nw_ref, conv0_ref, s0_ref,
              o_ref, conv_ref, state_ref, y_ref, *, steps):
    _, chunk, is_prompt = steps.locate(pl.program_id(0))

    @pl.when((chunk == 0) & is_prompt)
    def _():
        conv_ref[...] = jnp.zeros_like(conv_ref)
        state_ref[...] = jnp.zeros_like(state_ref)

    @pl.when((chunk == 0) & jnp.logical_not(is_prompt))
    def _():
        conv_ref[...] = conv0_ref[...]
        state_ref[...] = s0_ref[...]

    x = x_ref[...]
    full = jnp.concatenate([conv_ref[...], x], axis=0)
    conv = cw_ref[GDN_CONV - 1][None] * x
    for d in range(1, GDN_CONV):
        conv = conv + cw_ref[GDN_CONV - 1 - d][None] * full[CONV_PAD_ROWS - d:CONV_PAD_ROWS - d + CHUNK]
    conv_ref[...] = x[CHUNK - CONV_PAD_ROWS:]
    y = _silu(conv)
    n_qk = 2 * GDN_HEADS
    yqk = y[:, :n_qk, :]
    sq = (yqk * yqk).reshape(CHUNK * n_qk, GDN_HEAD_K).astype(BF16)
    ssq = _dot(sq, jnp.ones((GDN_HEAD_K, GDN_HEAD_K), BF16)).reshape(CHUNK, n_qk, GDN_HEAD_K)
    n_sl = y.shape[1]
    yn = jnp.concatenate([yqk * lax.rsqrt(ssq + 1e-6), y[:, n_qk:, :]], axis=1)
    y_ref[...] = yn.reshape(CHUNK * n_sl, GDN_HEAD_K)

    row = lax.broadcasted_iota(jnp.int32, (CHUNK, CHUNK), 0)
    col = lax.broadcasted_iota(jnp.int32, (CHUNK, CHUNK), 1)
    lower = row >= col
    strict = row > col
    eye = (row == col).astype(F32)
    tril = lower.astype(F32)

    slab = slab_ref[...]
    beta_all = _sigmoid(slab)
    lane = lax.broadcasted_iota(jnp.int32, slab.shape, 1)
    is_decay = (lane >= SLAB_DECAY) & (lane < SLAB_DECAY + GDN_HEADS)
    g_all = jnp.where(is_decay, -jnp.exp(alog_ref[...]) * _softplus(slab + dtb_ref[...]), 0.0)
    gc_all = jnp.dot(tril, g_all, precision=HIGHEST, preferred_element_type=F32)
    gc_rows = gc_all.T

    H = range(GDN_HEADS)
    def head_rows(s):
        return y_ref[pl.ds(s, CHUNK, stride=n_sl), :]

    qn = [head_rows(h) * (GDN_HEAD_K ** -0.5) for h in H]
    kn = [head_rows(GDN_HEADS + h) for h in H]
    vs = [head_rows(2 * GDN_HEADS + h) for h in H]
    gcc = [gc_all[:, SLAB_DECAY + h:SLAB_DECAY + h + 1] for h in H]
    gcr = [gc_rows[SLAB_DECAY + h:SLAB_DECAY + h + 1, :] for h in H]
    beta = [beta_all[:, SLAB_BETA + h:SLAB_BETA + h + 1] for h in H]
    egc = [jnp.exp(g) for g in gcc]
    g_last = [g[CHUNK - 1:CHUNK, :] for g in gcc]
    decay = [jnp.exp(jnp.where(lower, gcc[h] - gcr[h], -jnp.inf)) for h in H]
    kb = [k.astype(BF16) for k in kn]
    qb = [q.astype(BF16) for q in qn]
    state = [state_ref[h] for h in H]
    sb = [s.astype(BF16) for s in state]

    a = [_dot_nt(kb[h], kb[h]) * jnp.where(strict, decay[h], 0.0) * beta[h] for h in H]
    qk = [(_dot_nt(qb[h], kb[h]) * decay[h]).astype(BF16) for h in H]
    o_state = [_dot((qn[h] * egc[h]).astype(BF16), sb[h]) for h in H]
    t = [eye - a[h] for h in H]
    apb = [x.astype(BF16) for x in a]
    for _ in range(5):
        apb = [_dot(x, x).astype(BF16) for x in apb]
        t = [t[h] + _dot(t[h].astype(BF16), apb[h]) for h in H]
    rhs = [jnp.concatenate([vs[h] * beta[h], kn[h] * (beta[h] * egc[h])], axis=1).astype(BF16) for h in H]
    uw = [_dot(t[h].astype(BF16), rhs[h]) for h in H]
    v_new = [(uw[h][:, :GDN_HEAD_V] - _dot(uw[h][:, GDN_HEAD_V:].astype(BF16), sb[h])).astype(BF16) for h in H]
    o = [o_state[h] + _dot(qk[h], v_new[h]) for h in H]
    k_tail = [(kn[h] * jnp.exp(g_last[h] - gcc[h])).astype(BF16) for h in H]
    new_state = [state[h] * jnp.exp(g_last[h]) + _dot_tn(k_tail[h], v_new[h]) for h in H]
    for h in H:
        state_ref[h] = new_state[h]
    outs = []
    for h in H:
        ms = jnp.mean(o[h] * o[h], axis=-1, keepdims=True)
        zh = z_ref[:, h * GDN_HEAD_V:(h + 1) * GDN_HEAD_V]
        outs.append((o[h] * lax.rsqrt(ms + RMS_EPS) * nw_ref[...] * _silu(zh)).astype(o_ref.dtype))
    o_ref[...] = jnp.concatenate(outs, axis=1)


def _gdn(qkv, slab, z, conv_w, alog_row, dtb_row, norm_w, conv0, s0, *, steps):
    n_sl = GDN_CONV_CH // LANES
    n_seq = steps.bp + steps.bs
    conv_block = (None, CONV_PAD_ROWS, n_sl, LANES)
    state_block = (None, GDN_HEADS, GDN_HEAD_K, GDN_HEAD_V)
    seq_of = lambda s: steps.locate(s)[0]
    in_specs = [
        pl.BlockSpec((CHUNK, n_sl, LANES), lambda s: (s, 0, 0)),
        pl.BlockSpec((CHUNK, LANES), lambda s: (s, SLAB_COL // LANES)),
        pl.BlockSpec((CHUNK, GDN_V_W), lambda s: (s, 0)),
        pl.BlockSpec((GDN_CONV, n_sl, LANES), lambda s: (0, 0, 0)),
        pl.BlockSpec((1, LANES), lambda s: (0, 0)),
        pl.BlockSpec((1, LANES), lambda s: (0, 0)),
        pl.BlockSpec((1, GDN_HEAD_V), lambda s: (0, 0)),
        pl.BlockSpec(conv_block, lambda s: (steps.sample_index(s), 0, 0, 0)),
        pl.BlockSpec(state_block, lambda s: (steps.sample_index(s), 0, 0, 0)),
    ]
    out_specs = [
        pl.BlockSpec((CHUNK, GDN_V_W), lambda s: (s, 0)),
        pl.BlockSpec(conv_block, lambda s: (seq_of(s), 0, 0, 0)),
        pl.BlockSpec(state_block, lambda s: (seq_of(s), 0, 0, 0)),
    ]
    out_shape = [jax.ShapeDtypeStruct((steps.n_steps * CHUNK, GDN_V_W), BF16),
                 jax.ShapeDtypeStruct((n_seq, CONV_PAD_ROWS, n_sl, LANES), F32),
                 jax.ShapeDtypeStruct((n_seq, GDN_HEADS, GDN_HEAD_K, GDN_HEAD_V), F32)]
    o, conv, state = pl.pallas_call(
        functools.partial(_gdn_body, steps=steps), grid=(steps.n_steps,), in_specs=in_specs, out_specs=out_specs,
        out_shape=out_shape, scratch_shapes=[pltpu.VMEM((CHUNK * n_sl, LANES), F32)],
        compiler_params=_params("arbitrary"), name="gdn_chunk",
    )(qkv.reshape(-1, n_sl, LANES), slab, z, conv_w.reshape(GDN_CONV, n_sl, LANES), alog_row, dtb_row, norm_w,
      conv0.reshape(steps.bs, CONV_PAD_ROWS, n_sl, LANES), s0)
    return o, conv.reshape(n_seq, CONV_PAD_ROWS, GDN_CONV_CH), state


KV_SLAB_W = 1024
SLAB_COL = 2 * SWA_KV_W


def _split_w_in(w):
    wt = jnp.swapaxes(w, 1, 2)
    edges = [0]
    for s in IN_SIZES:
        edges.append(edges[-1] + s)
    rows = lambda a, b: wt[:, edges[a]:edges[b], :].astype(BF16)
    zeros = lambda n: jnp.zeros((w.shape[0], n, w.shape[1]), BF16)
    slab = [rows(6, 7), zeros(SLAB_BETA - SLAB_GLR - GLA_RANK), rows(9, 10),
            zeros(SLAB_DECAY - SLAB_BETA - GDN_HEADS), rows(10, 11)]
    used = SLAB_COL + SLAB_DECAY + GDN_HEADS
    kv_slab = jnp.concatenate([rows(1, 3)] + slab + [zeros(KV_SLAB_W - used)], axis=1)
    return dict(kv_slab=kv_slab, swa_q=rows(0, 1), gla_qkv=rows(3, 6), gla_r=rows(7, 8), gdn_qkv=rows(8, 9),
                gdn_z=rows(11, 12), gates=rows(12, 13))


def _swa_tables(sinks):
    tok = jnp.arange(CHUNK)
    key = jnp.arange(SWA_SEG)
    dist = jnp.abs((tok + SWA_WINDOW)[:, None] - key[None, :]).astype(F32)
    head = (jnp.arange(SWA_KV_HEADS)[:, None, None] * SWA_GROUP + jnp.arange(SWA_PAIRS)[None, :, None] * 2
            + jnp.arange(2)[None, None, :])
    slopes = 2.0 ** (-8.0 * (head + 1).astype(F32) / SWA_HEADS)
    bias = slopes[:, :, None, :, None] * dist[None, None, :, None, :]
    bias = jnp.where(key < SWA_KEYS, bias, SWA_PAD_BIAS)
    bias = bias.reshape(SWA_KV_HEADS, SWA_PAIRS * CHUNK, 2 * SWA_SEG)
    sink = jnp.repeat(sinks.astype(F32)[head], SWA_HEAD_DIM, axis=-1)
    sink = jnp.broadcast_to(sink[:, :, None, :], (SWA_KV_HEADS, SWA_PAIRS, CHUNK, LANES))
    return bias, sink.reshape(SWA_KV_HEADS, SWA_PAIRS * CHUNK, LANES)


def _key_rows(new, cache, *, batch, seq):
    t = new.reshape(batch, seq, SWA_KV_HEADS, SWA_HEAD_DIM)
    if cache is None:
        return jnp.pad(t, ((0, 0), (SWA_WINDOW, 0), (0, 0), (0, 0)))
    return jnp.concatenate([cache, t], axis=1)


def _pad_rows(t, n_rows):
    return jnp.pad(t, ((0, 0), (n_rows - t.shape[1], 0), (0, 0)))


def _layer(x, xb, pe_b, W, WB, layer, last, cache_k, cache_v, gla_s0, gdn_s0, conv_s0, *, bp, sp, bs, ss):
    n_prompt = bp * sp
    steps = _Steps(bp=bp, npc=sp // CHUNK, bs=bs, nsc=ss // CHUNK)
    win = WB['w_in']
    ident = _ep_identity
    proj = lambda group: _matmul(xb, win[group], layer, [F32], ident, w_rows=True, name="proj_" + group)[0]
    kv_slab = proj('kv_slab')
    swa_q = proj('swa_q')
    gla_qkv = proj('gla_qkv')
    gla_r = proj('gla_r')
    gdn_qkv = proj('gdn_qkv')
    gdn_z = proj('gdn_z')

    bias, sink = _swa_tables(W['swa_sinks'])
    new_kv, kv_all = [], []
    for col, cache in ((0, cache_k), (1, cache_v)):
        t = kv_slab[:, col * SWA_KV_W:(col + 1) * SWA_KV_W]
        rows_p = _key_rows(t[:n_prompt], None, batch=bp, seq=sp)
        rows_s = _key_rows(t[n_prompt:], cache, batch=bs, seq=ss)
        new_kv.append((rows_p[:, sp:], rows_s[:, ss:]))
        flat = jnp.concatenate([rows_p.reshape(-1, SWA_KV_HEADS, SWA_HEAD_DIM),
                                rows_s.reshape(-1, SWA_KV_HEADS, SWA_HEAD_DIM)], axis=0)
        flat = jnp.transpose(flat, (1, 0, 2)).astype(BF16)
        kv_all.append(jnp.concatenate([flat, flat], axis=-1))
    o_a = _swa(swa_q, kv_all[0], kv_all[1], bias, sink, steps=steps)

    wg_pad = jnp.zeros((LANES, GLA_K_W), F32).at[SLAB_GLR:SLAB_GLR + GLA_RANK].set(W['gla_w_gate2'])
    gbias = W['gla_gate_bias'].reshape(1, GLA_K_W)
    gnorm = W['gla_norm_w'].reshape(1, GLA_HEAD_V)
    o_b, gla_state = _gla(gla_qkv, kv_slab, gla_r, wg_pad, gbias, gnorm, gla_s0, steps=steps)

    alog_row = jnp.zeros((1, LANES), F32).at[0, SLAB_DECAY:SLAB_DECAY + GDN_HEADS].set(W['gdn_a_log'])
    dtb_row = jnp.zeros((1, LANES), F32).at[0, SLAB_DECAY:SLAB_DECAY + GDN_HEADS].set(W['gdn_dt_bias'])
    dnorm = W['gdn_norm_w'].reshape(1, GDN_HEAD_V)
    o_c, conv_state, gdn_state = _gdn(gdn_qkv, kv_slab, gdn_z, W['gdn_conv_w'], alog_row, dtb_row, dnorm,
                                      _pad_rows(conv_s0, CONV_PAD_ROWS), gdn_s0, steps=steps)
    conv_state = conv_state[:, CONV_PAD_ROWS - (GDN_CONV - 1):]

    merged = _merge(xb, win['gates'], jnp.stack([o_a, o_b, o_c]), WB['w_br'], layer)
    tm = _row_tile(x.shape[0], 512)
    tn = 512
    x, xb = _matmul_ln(merged, WB['w_out'], layer, W['ln1_g'], W['ln1_b'], _ep_residual,
                       [(x, *_tile_spec(tm, tn))], tm=tm, tn=tn, name="out_proj_ln")
    hid = _matmul(xb, WB['w_up'], layer, [BF16], _ep_relu_sq, name="ffn_up")[0]
    tm_down = _row_tile(x.shape[0], 1024)
    z = _matmul(hid, WB['w_down'], layer, [F32], _ep_residual, extras=[(x, *_tile_spec(tm_down, 1024))],
                tm=tm_down, tk=2048, name="ffn_down")[0]
    x, xb = _layer_norm(z, W['ln2_g'], W['ln2_b'])
    pe_extras = [(x, *_tile_spec(tm, tn)),
                 (pe_b, (tm, PE_DIM), lambda i, j, k: (i, 0)),
                 (WB['pe_w_proj'], (None, PE_DIM, tn), lambda i, j, k: (layer, 0, j))]
    pe_gate = functools.partial(_matmul_ln, xb, WB['pe_w_gate'], layer, W['ln3_g'], W['ln3_b'], _ep_pe, pe_extras,
                                tm=tm, tn=tn, name="pe_gate_ln")
    if last:
        x, xb = pe_gate(rows=(0, n_prompt))[0], pe_gate(rows=(n_prompt, x.shape[0] - n_prompt))[0]
    else:
        x, xb = pe_gate()
    states_p = (new_kv[0][0], new_kv[1][0], gla_state[:bp], gdn_state[:bp], conv_state[:bp])
    states_s = (new_kv[0][1], new_kv[1][1], gla_state[bp:], gdn_state[bp:], conv_state[bp:])
    return x, xb, states_p, states_s


def kernel(x_prompt, x_sample, cache_swa_k, cache_swa_v, state_gla, state_gdn, state_gdn_conv, p_prompt, p_sample, w_in, swa_sinks, gla_w_gate2, gla_gate_bias, gla_norm_w, gdn_conv_w, gdn_a_log, gdn_dt_bias, gdn_norm_w, w_br_swa, w_br_gla, w_br_gdn, w_out, ln1_g, ln1_b, w_up, w_down, ln2_g, ln2_b, pe_w_gate, pe_w_proj, ln3_g, ln3_b):
    bp, sp, d = x_prompt.shape
    bs, ss, _ = x_sample.shape
    n_prompt = bp * sp
    x = jnp.concatenate([x_prompt.reshape(n_prompt, d), x_sample.reshape(bs * ss, d)], axis=0)
    xb = x.astype(BF16)
    WB = {'w_in': _split_w_in(w_in), 'w_br': jnp.stack([w_br_swa, w_br_gla, w_br_gdn], axis=1).astype(BF16),
          'w_out': w_out.astype(BF16), 'w_up': w_up.astype(BF16), 'w_down': w_down.astype(BF16),
          'pe_w_gate': pe_w_gate.astype(BF16), 'pe_w_proj': pe_w_proj.astype(BF16)}
    st_p, st_s = [], []
    for l in range(DEPTH):
        W = {'swa_sinks': swa_sinks[l], 'gla_w_gate2': gla_w_gate2[l],
             'gla_gate_bias': gla_gate_bias[l], 'gla_norm_w': gla_norm_w[l], 'gdn_conv_w': gdn_conv_w[l],
             'gdn_a_log': gdn_a_log[l], 'gdn_dt_bias': gdn_dt_bias[l], 'gdn_norm_w': gdn_norm_w[l],
             'ln1_g': ln1_g[l], 'ln1_b': ln1_b[l], 'ln2_g': ln2_g[l], 'ln2_b': ln2_b[l],
             'ln3_g': ln3_g[l], 'ln3_b': ln3_b[l]}
        pe_b = jnp.concatenate([p_prompt[l].reshape(n_prompt, PE_DIM), p_sample[l].reshape(bs * ss, PE_DIM)],
                               axis=0).astype(BF16)
        x, xb, sp_l, ss_l = _layer(x, xb, pe_b, W, WB, l, l == DEPTH - 1, cache_swa_k[l], cache_swa_v[l],
                                   state_gla[l], state_gdn[l], state_gdn_conv[l], bp=bp, sp=sp, bs=bs, ss=ss)
        st_p.append(sp_l)
        st_s.append(ss_l)
    y_prompt = x.reshape(bp, sp, d)
    y_sample = xb.reshape(bs, ss, d)
    stack = lambda sts, i: jnp.stack([s[i] for s in sts])
    return (y_prompt, y_sample,
            stack(st_p, 0), stack(st_p, 1), stack(st_p, 2), stack(st_p, 3), stack(st_p, 4),
            stack(st_s, 0), stack(st_s, 1), stack(st_s, 2), stack(st_s, 3), stack(st_s, 4))
```
